```python
import jax, jax.numpy as jnp
from jax import lax
import numpy as np

D_MODEL = 1024
BATCH = 2
SEQ = 8192
DEPTH = 2
DEC_BATCH = 32
DEC_SEQ = 1
PAST_LEN = 8192
PAGE_SIZE = 128

N_MIXERS = 2
N_FOX_LAYERS = (DEPTH + 1) // 2
N_GLA_LAYERS = DEPTH // 2
FOX_HEADS = 16
FOX_HEAD_DIM = D_MODEL // FOX_HEADS
FOX_Q_BLOCK = 128
FOX_FGATE_BIAS = 3.0
GLA_HEADS = 4
GLA_DK = D_MODEL // 2
GLA_DV = D_MODEL
GLA_DK_H = GLA_DK // GLA_HEADS
GLA_DV_H = GLA_DV // GLA_HEADS
GLA_RANK = 16
GLA_TAU = 16.0
GLA_CHUNK = 64
D_FF = ((8 * D_MODEL // 3 + 127) // 128) * 128
DN_ALPHA = (2 * DEPTH) ** 0.25
DN_BETA = (8 * DEPTH) ** -0.25
LN_EPS = 1e-5
MASK_VALUE = -1e30

kernel_name = "fox_gla_macaron_deepnorm_step"


def layer_norm(x, g, b):
    xf = x.astype(jnp.float32)
    mu = jnp.mean(xf, axis=-1, keepdims=True)
    var = jnp.mean(jnp.square(xf - mu), axis=-1, keepdims=True)
    y = (xf - mu) * lax.rsqrt(var + LN_EPS)
    return (y * g.astype(jnp.float32) + b.astype(jnp.float32)).astype(x.dtype)


def swiglu(x, w_in, w_out):
    g, u = jnp.split(x @ w_in, 2, axis=-1)
    return (jax.nn.silu(g) * u) @ w_out


def macaron_half(x, w_in, w_out, g, b):
    return layer_norm(DN_ALPHA * x + 0.5 * swiglu(x, w_in, w_out), g, b)


def fox_project(x, w_in, b_f):
    B, T, _ = x.shape
    q, k, v, f = jnp.split(x @ w_in, [D_MODEL, 2 * D_MODEL, 3 * D_MODEL], axis=-1)
    hs = (B, T, FOX_HEADS, FOX_HEAD_DIM)
    logf = jax.nn.log_sigmoid((f + b_f).astype(jnp.float32))
    return q.reshape(hs), k.reshape(hs), v.reshape(hs), logf


def fox_prompt(q, k, v, logf):
    B, T, H, HD = q.shape
    scale = HD ** -0.5
    c_k = jnp.cumsum(logf, axis=1).transpose(0, 2, 1)
    k_pos = jnp.arange(T)

    def block(i):
        start = i * FOX_Q_BLOCK
        q_blk = lax.dynamic_slice_in_dim(q, start, FOX_Q_BLOCK, axis=1)
        c_q = lax.dynamic_slice_in_dim(c_k, start, FOX_Q_BLOCK, axis=2)
        logits = (jnp.einsum('bqhd,bkhd->bhqk', q_blk, k).astype(jnp.float32) * scale
                  + (c_q[..., None] - c_k[:, :, None, :]))
        q_pos = start + jnp.arange(FOX_Q_BLOCK)
        logits = jnp.where(k_pos[None, :] <= q_pos[:, None], logits, MASK_VALUE)
        p = jax.nn.softmax(logits, axis=-1).astype(v.dtype)
        return jnp.einsum('bhqk,bkhd->bqhd', p, v)

    out = lax.map(block, jnp.arange(T // FOX_Q_BLOCK))
    return out.transpose(1, 0, 2, 3, 4).reshape(B, T, H * HD)


def fox_sample(q, k, v, logf, ck, cv, clogf, page_table):
    DB, S, H, HD = q.shape
    scale = HD ** -0.5
    past_k = ck[page_table].reshape(DB, -1, H, HD)
    past_v = cv[page_table].reshape(DB, -1, H, HD)
    past_lf = clogf[page_table].reshape(DB, -1, H).astype(jnp.float32)
    past_len = past_k.shape[1]
    keys = jnp.concatenate([past_k, k.astype(past_k.dtype)], axis=1)
    vals = jnp.concatenate([past_v, v.astype(past_v.dtype)], axis=1)
    lf = jnp.concatenate([past_lf, logf], axis=1)
    csum = jnp.cumsum(lf, axis=1).transpose(0, 2, 1)
    c_q = csum[:, :, past_len:]
    logits = (jnp.einsum('bqhd,bkhd->bhqk', q.astype(keys.dtype), keys).astype(jnp.float32) * scale
              + (c_q[..., None] - csum[:, :, None, :]))
    q_pos = past_len + jnp.arange(S)
    k_pos = jnp.arange(keys.shape[1])
    logits = jnp.where(k_pos[None, :] <= q_pos[:, None], logits, MASK_VALUE)
    p = jax.nn.softmax(logits, axis=-1).astype(vals.dtype)
    out = jnp.einsum('bhqk,bkhd->bqhd', p, vals)
    return out.reshape(DB, S, H * HD).astype(q.dtype)


def gla_project(x, w_in, w_a2, b_a):
    B, T, _ = x.shape
    q, k, v, r, a_lr = jnp.split(
        x @ w_in, [GLA_DK, 2 * GLA_DK, 2 * GLA_DK + GLA_DV, 2 * GLA_DK + 2 * GLA_DV], axis=-1)
    q = q.reshape(B, T, GLA_HEADS, GLA_DK_H) * (GLA_DK_H ** -0.5)
    k = k.reshape(B, T, GLA_HEADS, GLA_DK_H)
    v = v.reshape(B, T, GLA_HEADS, GLA_DV_H)
    log_a = jax.nn.log_sigmoid((a_lr @ w_a2 + b_a).astype(jnp.float32)) / GLA_TAU
    return q, k, v, r, log_a.reshape(B, T, GLA_HEADS, GLA_DK_H)


def gla_chunked(q, k, v, log_a, s0, chunk):
    B, T, H, DK = q.shape
    DV = v.shape[-1]
    N = T // chunk

    def to_chunks(a):
        return a.astype(jnp.float32).reshape(B, N, chunk, H, a.shape[-1]).transpose(1, 0, 3, 2, 4)

    qc, kc, vc = to_chunks(q), to_chunks(k), to_chunks(v)
    bc = jnp.cumsum(to_chunks(log_a), axis=3)
    b_last = bc[:, :, :, -1:, :]
    q_dec = qc * jnp.exp(bc)
    k_inv = kc * jnp.exp(-bc)
    k_end = kc * jnp.exp(b_last - bc)
    tril = jnp.tril(jnp.ones((chunk, chunk), dtype=bool))
    attn = jnp.where(tril, jnp.einsum('nbhid,nbhjd->nbhij', q_dec, k_inv), 0.0)
    o_intra = jnp.einsum('nbhij,nbhjv->nbhiv', attn, vc)

    def step(s, xs):
        qd, ke, vv, bl = xs
        o_inter = jnp.einsum('bhid,bhdv->bhiv', qd, s)
        s_new = jnp.exp(bl[:, :, 0, :])[..., None] * s + jnp.einsum('bhjd,bhjv->bhdv', ke, vv)
        return s_new, o_inter

    s_final, o_inter = lax.scan(step, s0.astype(jnp.float32), (q_dec, k_end, vc, b_last))
    o = (o_intra + o_inter).transpose(1, 0, 3, 2, 4).reshape(B, T, H, DV)
    return o, s_final


def gla_output(o, r, g, w_o):
    mu = jnp.mean(o, axis=-1, keepdims=True)
    var = jnp.mean(jnp.square(o - mu), axis=-1, keepdims=True)
    on = ((o - mu) * lax.rsqrt(var + LN_EPS)).reshape(o.shape[0], o.shape[1], GLA_DV)
    on = (on * g.astype(jnp.float32)).astype(r.dtype)
    return (on * jax.nn.silu(r)) @ w_o


def setup_inputs(seed: int = 0) -> dict:
    key = jax.random.key(seed)
    ks = jax.random.split(key, 20)
    n_pages = PAST_LEN // PAGE_SIZE
    n_pool = (DEC_BATCH * n_pages * 5) // 4
    f32 = jnp.float32
    nrm = lambda k, s: jax.random.normal(k, s, dtype=f32)

    x_prompt = nrm(ks[0], (BATCH, SEQ, D_MODEL))
    x_sample = nrm(ks[1], (DEC_BATCH, DEC_SEQ, D_MODEL))
    cache_fox_k = nrm(ks[2], (N_FOX_LAYERS, n_pool, PAGE_SIZE, FOX_HEADS, FOX_HEAD_DIM))
    cache_fox_v = DN_BETA * nrm(ks[3], (N_FOX_LAYERS, n_pool, PAGE_SIZE, FOX_HEADS, FOX_HEAD_DIM))
    cache_fox_logf = jax.nn.log_sigmoid(
        FOX_FGATE_BIAS + nrm(ks[4], (N_FOX_LAYERS, n_pool, PAGE_SIZE, FOX_HEADS)))
    state_gla = 0.5 * nrm(ks[5], (N_GLA_LAYERS, DEC_BATCH, GLA_HEADS, GLA_DK_H, GLA_DV_H))
    page_table = jax.random.permutation(ks[6], n_pool)[:DEC_BATCH * n_pages].reshape(
        DEC_BATCH, n_pages).astype(jnp.int32)

    ln_g = 1.0 + 0.02 * nrm(ks[7], (DEPTH, 3, D_MODEL))
    ln_b = 0.02 * nrm(ks[8], (DEPTH, 3, D_MODEL))
    ffn_w_in = nrm(ks[9], (DEPTH, 2, D_MODEL, 2 * D_FF)) * D_MODEL ** -0.5
    ffn_w_out = nrm(ks[10], (DEPTH, 2, D_FF, D_MODEL)) * (D_FF ** -0.5 * DN_BETA)

    fox_cols = jnp.concatenate([jnp.ones((2 * D_MODEL,), f32), jnp.full((D_MODEL,), DN_BETA, f32),
                                jnp.ones((FOX_HEADS,), f32)])
    fox_w_in = nrm(ks[11], (N_FOX_LAYERS, D_MODEL, 3 * D_MODEL + FOX_HEADS)) * D_MODEL ** -0.5 * fox_cols
    fox_b_f = FOX_FGATE_BIAS + 0.1 * nrm(ks[12], (N_FOX_LAYERS, FOX_HEADS))
    fox_w_o = nrm(ks[13], (N_FOX_LAYERS, D_MODEL, D_MODEL)) * (D_MODEL ** -0.5 * DN_BETA)

    gla_cols = jnp.concatenate([jnp.ones((2 * GLA_DK,), f32), jnp.full((GLA_DV,), DN_BETA, f32),
                                jnp.ones((GLA_DV + GLA_RANK,), f32)])
    gla_w_in = nrm(ks[14], (N_GLA_LAYERS, D_MODEL, 2 * GLA_DK + 2 * GLA_DV + GLA_RANK)) * D_MODEL ** -0.5 * gla_cols
    gla_w_a2 = nrm(ks[15], (N_GLA_LAYERS, GLA_RANK, GLA_DK)) * GLA_RANK ** -0.5
    gla_b_a = 0.1 * nrm(ks[16], (N_GLA_LAYERS, GLA_DK))
    gla_norm_g = 1.0 + 0.02 * nrm(ks[17], (N_GLA_LAYERS, GLA_DV))
    gla_w_o = nrm(ks[18], (N_GLA_LAYERS, GLA_DV, D_MODEL)) * (GLA_DV ** -0.5 * DN_BETA)

    return {"x_prompt": x_prompt, "x_sample": x_sample,
            "cache_fox_k": cache_fox_k, "cache_fox_v": cache_fox_v, "cache_fox_logf": cache_fox_logf,
            "state_gla": state_gla, "page_table": page_table,
            "ln_g": ln_g, "ln_b": ln_b, "ffn_w_in": ffn_w_in, "ffn_w_out": ffn_w_out,
            "fox_w_in": fox_w_in, "fox_b_f": fox_b_f, "fox_w_o": fox_w_o,
            "gla_w_in": gla_w_in, "gla_w_a2": gla_w_a2, "gla_b_a": gla_b_a,
            "gla_norm_g": gla_norm_g, "gla_w_o": gla_w_o}


def reference(x_prompt, x_sample, cache_fox_k, cache_fox_v, cache_fox_logf, state_gla, page_table,
              ln_g, ln_b, ffn_w_in, ffn_w_out, fox_w_in, fox_b_f, fox_w_o,
              gla_w_in, gla_w_a2, gla_b_a, gla_norm_g, gla_w_o):
    xp, xs = x_prompt, x_sample
    kp_l, vp_l, lfp_l, ks_l, vs_l, lfs_l = [], [], [], [], [], []
    sgp_l, sgs_l = [], []
    for i in range(DEPTH):
        j = i // N_MIXERS
        xp = macaron_half(xp, ffn_w_in[i, 0], ffn_w_out[i, 0], ln_g[i, 0], ln_b[i, 0])
        xs = macaron_half(xs, ffn_w_in[i, 0], ffn_w_out[i, 0], ln_g[i, 0], ln_b[i, 0])
        if i % N_MIXERS == 0:
            qp, kp, vp, lfp = fox_project(xp, fox_w_in[j], fox_b_f[j])
            qs, kss, vss, lfs = fox_project(xs, fox_w_in[j], fox_b_f[j])
            mp = fox_prompt(qp, kp, vp, lfp) @ fox_w_o[j]
            ms = fox_sample(qs, kss, vss, lfs, cache_fox_k[j], cache_fox_v[j], cache_fox_logf[j],
                            page_table) @ fox_w_o[j]
            kp_l.append(kp); vp_l.append(vp); lfp_l.append(lfp)
            ks_l.append(kss); vs_l.append(vss); lfs_l.append(lfs)
        else:
            qp, kp, vp, rp, lap = gla_project(xp, gla_w_in[j], gla_w_a2[j], gla_b_a[j])
            qs, kss, vss, rs, las = gla_project(xs, gla_w_in[j], gla_w_a2[j], gla_b_a[j])
            s0p = jnp.zeros((xp.shape[0], GLA_HEADS, GLA_DK_H, GLA_DV_H), jnp.float32)
            op, sp = gla_chunked(qp, kp, vp, lap, s0p, GLA_CHUNK)
            os_, ss = gla_chunked(qs, kss, vss, las, state_gla[j], xs.shape[1])
            mp = gla_output(op, rp, gla_norm_g[j], gla_w_o[j])
            ms = gla_output(os_, rs, gla_norm_g[j], gla_w_o[j])
            sgp_l.append(sp); sgs_l.append(ss)
        xp = layer_norm(DN_ALPHA * xp + mp, ln_g[i, 1], ln_b[i, 1])
        xs = layer_norm(DN_ALPHA * xs + ms, ln_g[i, 1], ln_b[i, 1])
        xp = macaron_half(xp, ffn_w_in[i, 1], ffn_w_out[i, 1], ln_g[i, 2], ln_b[i, 2])
        xs = macaron_half(xs, ffn_w_in[i, 1], ffn_w_out[i, 1], ln_g[i, 2], ln_b[i, 2])
    new_k_prompt = jnp.stack(kp_l, axis=0)
    new_v_prompt = jnp.stack(vp_l, axis=0)
    new_logf_prompt = jnp.stack(lfp_l, axis=0)
    new_k_sample = jnp.stack(ks_l, axis=0)
    new_v_sample = jnp.stack(vs_l, axis=0)
    new_logf_sample = jnp.stack(lfs_l, axis=0)
    state_gla_prompt = jnp.stack(sgp_l, axis=0)
    state_gla_sample = jnp.stack(sgs_l, axis=0)
    return (xp, xs, new_k_prompt, new_v_prompt, new_logf_prompt,
            new_k_sample, new_v_sample, new_logf_sample, state_gla_prompt, state_gla_sample)
```

```python
import functools

import numpy as np
import jax
import jax.numpy as jnp
from jax import lax
from jax.experimental import pallas as pl
from jax.experimental.pallas import tpu as pltpu

F32 = jnp.float32
BF16 = jnp.bfloat16

D_MODEL = 1024
DEPTH = 2
PAGE_SIZE = 128
FOX_HEADS = 16
FOX_HEAD_DIM = D_MODEL // FOX_HEADS
FOX_PAIRS = FOX_HEADS // 2
GLA_HEADS = 4
GLA_DK = D_MODEL // 2
GLA_DV = D_MODEL
GLA_DK_H = GLA_DK // GLA_HEADS
GLA_DV_H = GLA_DV // GLA_HEADS
GLA_RANK = 16
GLA_TAU = 16.0
GLA_CHUNK = 64
D_FF = ((8 * D_MODEL // 3 + 127) // 128) * 128
DN_ALPHA = (2 * DEPTH) ** 0.25
LN_EPS = 1e-5
MASK_VALUE = -1e30

LANES = 128
SUBLANES = 8
MXU_WIDTH = 256
FF_CHUNK = MXU_WIDTH
AUG_PARTS = 3
VMEM_LIMIT = 56 * 2 ** 20


def _params(*sem):
    return pltpu.CompilerParams(dimension_semantics=sem, vmem_limit_bytes=VMEM_LIMIT)


def _resident(shape):
    return pl.BlockSpec(shape, lambda *_: (0,) * len(shape), pipeline_mode=pl.Buffered(1))


def _dot(a, b):
    return jnp.dot(a, b, preferred_element_type=F32)


def _dot_nt(a, b):
    return lax.dot_general(a, b, (((1,), (1,)), ((), ())), preferred_element_type=F32)


def _dot_tn(a, b):
    return lax.dot_general(a, b, (((0,), (0,)), ((), ())), preferred_element_type=F32)


def _layer_norm(y, g, b):
    mu = jnp.mean(y, axis=-1, keepdims=True)
    d = y - mu
    var = jnp.mean(d * d, axis=-1, keepdims=True)
    return d * lax.rsqrt(var + LN_EPS) * g + b


def _log_sigmoid(z):
    return jnp.minimum(z, 0.0) - jnp.log1p(jnp.exp(-jnp.abs(z)))


def _split3(x):
    hi = x.astype(BF16).astype(F32)
    r = x - hi
    mid = r.astype(BF16).astype(F32)
    lo = (r - mid).astype(BF16).astype(F32)
    return hi, mid, lo


def _tri_dot(tri, x):
    hi, mid, lo = _split3(x)
    return (_dot(tri, hi.astype(BF16)) + _dot(tri, mid.astype(BF16))) + _dot(tri, lo.astype(BF16))


def _dot_tri(x, tri):
    hi, mid, lo = _split3(x)
    return (_dot(hi.astype(BF16), tri) + _dot(mid.astype(BF16), tri)) + _dot(lo.astype(BF16), tri)


def _swiglu_ln(x, win_ref, wout_ref, g, b):
    xb = x.astype(BF16)
    acc = jnp.zeros(x.shape, F32)
    for c in range(D_FF // FF_CHUNK):
        lo = c * FF_CHUNK
        gate = _dot(xb, win_ref[:, lo:lo + FF_CHUNK])
        up = _dot(xb, win_ref[:, D_FF + lo:D_FF + lo + FF_CHUNK])
        h = (gate * jax.nn.sigmoid(gate) * up).astype(BF16)
        acc = acc + _dot(h, wout_ref[lo:lo + FF_CHUNK, :])
    return _layer_norm(DN_ALPHA * x + 0.5 * acc, g, b)


def _ffn_kernel(x_ref, win_ref, wout_ref, g_ref, b_ref, o_ref):
    o_ref[...] = _swiglu_ln(x_ref[...], win_ref, wout_ref, g_ref[...], b_ref[...])


def _ffn_half(x, w_in, w_out, g, b, tm):
    n = x.shape[0]
    row = pl.BlockSpec((tm, D_MODEL), lambda i: (i, 0))
    return pl.pallas_call(
        _ffn_kernel,
        grid=(n // tm,),
        in_specs=[row, _resident(w_in.shape), _resident(w_out.shape),
                  _resident((1, D_MODEL)), _resident((1, D_MODEL))],
        out_specs=row,
        out_shape=jax.ShapeDtypeStruct((n, D_MODEL), F32),
        compiler_params=_params("parallel"),
        name="ffn_half",
    )(x, w_in, w_out, g, b)


def _mix_ffn_kernel(x_ref, m_ref, wo_ref, g1_ref, b1_ref, win_ref, wout_ref, g2_ref, b2_ref, o_ref):
    x1 = _layer_norm(DN_ALPHA * x_ref[...] + _dot(m_ref[...], wo_ref[...]), g1_ref[...], b1_ref[...])
    o_ref[...] = _swiglu_ln(x1, win_ref, wout_ref, g2_ref[...], b2_ref[...])


def _gla_mix_ffn_kernel(x_ref, att_ref, r_ref, gn_ref, wo_ref, g1_ref, b1_ref,
                        win_ref, wout_ref, g2_ref, b2_ref, o_ref):
    heads = []
    for h in range(GLA_HEADS):
        o = att_ref[:, h * GLA_DV_H:(h + 1) * GLA_DV_H]
        mu = jnp.mean(o, axis=-1, keepdims=True)
        d = o - mu
        var = jnp.mean(d * d, axis=-1, keepdims=True)
        heads.append(d * lax.rsqrt(var + LN_EPS))
    on = jnp.concatenate(heads, axis=-1) * gn_ref[...]
    r = r_ref[...]
    m = (on * (r * jax.nn.sigmoid(r))).astype(BF16)
    x1 = _layer_norm(DN_ALPHA * x_ref[...] + _dot(m, wo_ref[...]), g1_ref[...], b1_ref[...])
    o_ref[...] = _swiglu_ln(x1, win_ref, wout_ref, g2_ref[...], b2_ref[...])


def _mix_ffn(x, m, w_o, g1, b1, w_in, w_out, g2, b2, tm):
    n = x.shape[0]
    row = pl.BlockSpec((tm, D_MODEL), lambda i: (i, 0))
    vec = _resident((1, D_MODEL))
    return pl.pallas_call(
        _mix_ffn_kernel,
        grid=(n // tm,),
        in_specs=[row, row, _resident(w_o.shape), vec, vec,
                  _resident(w_in.shape), _resident(w_out.shape), vec, vec],
        out_specs=row,
        out_shape=jax.ShapeDtypeStruct((n, D_MODEL), F32),
        compiler_params=_params("parallel"),
        name="mix_ffn",
    )(x, m, w_o, g1, b1, w_in, w_out, g2, b2)


def _gla_mix_ffn(x, o, r, gn, w_o, g1, b1, w_in, w_out, g2, b2, tm):
    n = x.shape[0]
    row = pl.BlockSpec((tm, D_MODEL), lambda i: (i, 0))
    vec = _resident((1, D_MODEL))
    return pl.pallas_call(
        _gla_mix_ffn_kernel,
        grid=(n // tm,),
        in_specs=[row, row, row, vec, _resident(w_o.shape), vec, vec,
                  _resident(w_in.shape), _resident(w_out.shape), vec, vec],
        out_specs=row,
        out_shape=jax.ShapeDtypeStruct((n, D_MODEL), F32),
        compiler_params=_params("parallel"),
        name="gla_mix_ffn",
    )(x, o, r, gn, w_o, g1, b1, w_in, w_out, g2, b2)


def _fox_proj_kernel(x_ref, wqkv_ref, wf_ref, bf_ref, q_ref, k_ref, v_ref, kb_ref, vb_ref, lf_ref):
    xb = x_ref[...].astype(BF16)
    q = _dot(xb, wqkv_ref[:, 0:D_MODEL])
    q_ref[...] = (q * FOX_HEAD_DIM ** -0.5).astype(BF16)
    k = _dot(xb, wqkv_ref[:, D_MODEL:2 * D_MODEL])
    k_ref[...] = k
    kb_ref[...] = k.astype(BF16)
    v = _dot(xb, wqkv_ref[:, 2 * D_MODEL:3 * D_MODEL])
    v_ref[...] = v
    vb_ref[...] = v.astype(BF16)
    lf_ref[...] = _log_sigmoid(_dot(xb, wf_ref[...]) + bf_ref[...])


def _fox_proj(x, w_qkv, w_f, b_f, tm):
    n = x.shape[0]
    row = pl.BlockSpec((tm, D_MODEL), lambda i: (i, 0))
    gate = pl.BlockSpec((tm, FOX_HEADS), lambda i: (i, 0))
    full = lambda dt: jax.ShapeDtypeStruct((n, D_MODEL), dt)
    return pl.pallas_call(
        _fox_proj_kernel,
        grid=(n // tm,),
        in_specs=[row, _resident(w_qkv.shape), _resident(w_f.shape), _resident((1, FOX_HEADS))],
        out_specs=[row, row, row, row, row, gate],
        out_shape=[full(BF16), full(F32), full(F32), full(BF16), full(BF16),
                   jax.ShapeDtypeStruct((n, FOX_HEADS), F32)],
        compiler_params=_params("parallel"),
        name="fox_proj",
    )(x, w_qkv, w_f, b_f)


def _aug_placement():
    pq = np.zeros((AUG_PARTS, FOX_HEADS, D_MODEL), np.float32)
    pk = np.zeros((AUG_PARTS, FOX_HEADS, D_MODEL), np.float32)
    ones_q = np.zeros((1, D_MODEL), np.float32)
    ones_k = np.zeros((1, D_MODEL), np.float32)
    for h in range(FOX_HEADS):
        base = (h // 2) * LANES + (FOX_HEAD_DIM if h % 2 == 0 else 0)
        for j in range(AUG_PARTS):
            pq[j, h, base + AUG_PARTS + j] = 1.0
            pk[j, h, base + j] = -1.0
            ones_q[0, base + j] = 1.0
            ones_k[0, base + AUG_PARTS + j] = 1.0
    return pq, pk, ones_q, ones_k


def _fox_aug_kernel(q_ref, k_ref, lf_ref, tri_ref, pq_ref, pk_ref, oq_ref, ok_ref,
                    qa_ref, ka_ref, carry_ref):
    @pl.when(pl.program_id(1) == 0)
    def _():
        carry_ref[...] = jnp.zeros_like(carry_ref)

    tm = q_ref.shape[0]
    c = _tri_dot(tri_ref[...], lf_ref[...]) + carry_ref[...]
    carry_ref[...] = c[tm - 1:tm, :]
    parts = [p.astype(BF16) for p in _split3(c)]
    aug_q = oq_ref[...]
    aug_k = ok_ref[...]
    for j in range(AUG_PARTS):
        aug_q = aug_q + _dot(parts[j], pq_ref[j])
        aug_k = aug_k + _dot(parts[j], pk_ref[j])
    lane = lax.broadcasted_iota(jnp.int32, (tm, D_MODEL), 1) & (LANES - 1)
    low = lane < FOX_HEAD_DIM
    q = q_ref[...].astype(F32)
    k = k_ref[...].astype(F32)
    q_even = jnp.where(low, q, aug_q).astype(BF16)
    q_odd = jnp.where(low, aug_q, q).astype(BF16)
    k_even = jnp.where(low, k, aug_k).astype(BF16)
    k_odd = jnp.where(low, aug_k, k).astype(BF16)
    for p in range(FOX_PAIRS):
        src = slice(p * LANES, (p + 1) * LANES)
        qa_ref[:, 2 * p * LANES:(2 * p + 1) * LANES] = q_even[:, src]
        qa_ref[:, (2 * p + 1) * LANES:(2 * p + 2) * LANES] = q_odd[:, src]
        ka_ref[:, 2 * p * LANES:(2 * p + 1) * LANES] = k_even[:, src]
        ka_ref[:, (2 * p + 1) * LANES:(2 * p + 2) * LANES] = k_odd[:, src]


def _fox_aug(q, kb, lf, batch, seq, tm):
    pq, pk, ones_q, ones_k = _aug_placement()
    tri = jnp.asarray(np.tril(np.ones((tm, tm), np.float32)), BF16)
    nt = seq // tm
    row = pl.BlockSpec((tm, D_MODEL), lambda b, t: (b * nt + t, 0))
    gate = pl.BlockSpec((tm, FOX_HEADS), lambda b, t: (b * nt + t, 0))
    wide = pl.BlockSpec((tm, 2 * D_MODEL), lambda b, t: (b * nt + t, 0))
    out = jax.ShapeDtypeStruct((batch * seq, 2 * D_MODEL), BF16)
    return pl.pallas_call(
        _fox_aug_kernel,
        grid=(batch, nt),
        in_specs=[row, row, gate, _resident((tm, tm)), _resident(pq.shape), _resident(pk.shape),
                  _resident((1, D_MODEL)), _resident((1, D_MODEL))],
        out_specs=[wide, wide],
        out_shape=[out, out],
        scratch_shapes=[pltpu.VMEM((1, FOX_HEADS), F32)],
        compiler_params=_params("parallel", "arbitrary"),
        name="fox_aug",
    )(q, kb, lf, tri, jnp.asarray(pq, BF16), jnp.asarray(pk, BF16), jnp.asarray(ones_q), jnp.asarray(ones_k))


def _fox_attn_kernel(qa_ref, ka_ref, v_ref, o_ref, *, tq):
    i = pl.program_id(2)
    q_even = qa_ref[:, 0:LANES]
    q_odd = qa_ref[:, LANES:2 * LANES]

    def head(q, k, vv, state, mask):
        m, l, acc = state
        s = _dot_nt(q, k)
        if mask is not None:
            s = jnp.where(mask, s, MASK_VALUE)
        m_new = jnp.maximum(m, jnp.max(s, axis=1, keepdims=True))
        corr = jnp.exp(m - m_new)
        p = jnp.exp(s - m_new)
        l = corr * l + jnp.sum(p, axis=1, keepdims=True)
        acc = corr * acc + _dot(p.astype(BF16), vv)
        return m_new, l, acc

    def block(j, states, mask):
        start = pl.multiple_of(j * tq, tq)
        k_even = ka_ref[pl.ds(start, tq), 0:LANES]
        k_odd = ka_ref[pl.ds(start, tq), LANES:2 * LANES]
        vv = v_ref[pl.ds(start, tq), :]
        return (head(q_even, k_even, vv, states[0], mask), head(q_odd, k_odd, vv, states[1], mask))

    init = (jnp.full((tq, 1), MASK_VALUE, F32), jnp.zeros((tq, 1), F32), jnp.zeros((tq, LANES), F32))
    states = lax.fori_loop(0, i, lambda j, st: block(j, st, None), (init, init))
    causal = (lax.broadcasted_iota(jnp.int32, (tq, tq), 1) <= lax.broadcasted_iota(jnp.int32, (tq, tq), 0))
    (_, l0, a0), (_, l1, a1) = block(i, states, causal)
    lane = lax.broadcasted_iota(jnp.int32, (tq, LANES), 1)
    o_ref[...] = jnp.where(lane < FOX_HEAD_DIM, a0 * (1.0 / l0), a1 * (1.0 / l1)).astype(BF16)


def _fox_attn(qa, ka, vb, batch, seq, tq):
    nq = seq // tq
    return pl.pallas_call(
        functools.partial(_fox_attn_kernel, tq=tq),
        grid=(batch, FOX_PAIRS, nq),
        in_specs=[pl.BlockSpec((tq, 2 * LANES), lambda b, p, i: (b * nq + i, p)),
                  pl.BlockSpec((seq, 2 * LANES), lambda b, p, i: (b, p)),
                  pl.BlockSpec((seq, LANES), lambda b, p, i: (b, p))],
        out_specs=pl.BlockSpec((tq, LANES), lambda b, p, i: (b * nq + i, p)),
        out_shape=jax.ShapeDtypeStruct((batch * seq, D_MODEL), BF16),
        compiler_params=_params("parallel", "parallel", "arbitrary"),
        name="fox_attn",
    )(qa, ka, vb)


def _head_rows():
    sel = np.zeros((FOX_HEADS, D_MODEL), np.float32)
    for h in range(FOX_HEADS):
        sel[h, h * FOX_HEAD_DIM:(h + 1) * FOX_HEAD_DIM] = 1.0
    return sel


def _fox_decode_kernel(pt_ref, q_ref, kn_ref, vn_ref, lfn_ref, ck_ref, cv_ref, clf_ref,
                       sel_ref, later_ref, o_ref,
                       qrows_ref, m_ref, l_ref, acc_ref, suf_ref):
    j = pl.program_id(1)

    @pl.when(j == 0)
    def _():
        qrows = (sel_ref[...] * q_ref[0]).astype(BF16)
        qrows_ref[...] = qrows
        kn = jnp.broadcast_to(kn_ref[0], (SUBLANES, D_MODEL)).astype(BF16)
        m_ref[...] = _dot_nt(qrows, kn)[:, 0:1]
        l_ref[...] = jnp.ones_like(l_ref)
        acc_ref[...] = jnp.broadcast_to(vn_ref[0].astype(BF16).astype(F32), acc_ref.shape)
        suf_ref[...] = lfn_ref[0]

    lf = clf_ref[0]
    bias = _dot_tri(lf, later_ref[...]) + suf_ref[...]
    s = _dot(qrows_ref[...], ck_ref[0].astype(BF16)) + bias
    m_old = m_ref[...]
    m_new = jnp.maximum(m_old, jnp.max(s, axis=1, keepdims=True))
    corr = jnp.exp(m_old - m_new)
    p = jnp.exp(s - m_new)
    l_ref[...] = corr * l_ref[...] + jnp.sum(p, axis=1, keepdims=True)
    m_ref[...] = m_new
    acc_ref[...] = corr * acc_ref[...] + _dot_nt(p.astype(BF16), cv_ref[0].astype(BF16))
    suf_ref[...] = suf_ref[...] + jnp.sum(lf, axis=1, keepdims=True)

    @pl.when(j == pl.num_programs(1) - 1)
    def _():
        out = jnp.sum(sel_ref[...] * acc_ref[...] * (1.0 / l_ref[...]), axis=0, keepdims=True)
        o_ref[0] = out.astype(BF16)


def _fox_decode(q, k_new, v_new, lf_new, cache_k, cache_v, cache_lf, page_table):
    nb, n_pages = page_table.shape
    n_pool = cache_k.shape[0]
    later = jnp.asarray(np.tril(np.ones((PAGE_SIZE, PAGE_SIZE), np.float32), -1), BF16)
    kt = jnp.transpose(cache_k, (0, 2, 3, 1)).reshape(n_pool, D_MODEL, PAGE_SIZE)
    vt = jnp.transpose(cache_v, (0, 2, 3, 1)).reshape(n_pool, D_MODEL, PAGE_SIZE)
    lft = jnp.transpose(cache_lf, (0, 2, 1))
    page = lambda b, j, pt: (pt[b, n_pages - 1 - j], 0, 0)
    whole = lambda shape: pl.BlockSpec(shape, lambda b, j, pt: (0,) * len(shape))
    per_seq = lambda *shape: pl.BlockSpec((1,) + shape, lambda b, j, pt: (b, 0, 0))
    grid_spec = pltpu.PrefetchScalarGridSpec(
        num_scalar_prefetch=1,
        grid=(nb, n_pages),
        in_specs=[per_seq(1, D_MODEL), per_seq(1, D_MODEL), per_seq(1, D_MODEL), per_seq(FOX_HEADS, 1),
                  pl.BlockSpec((1, D_MODEL, PAGE_SIZE), page),
                  pl.BlockSpec((1, D_MODEL, PAGE_SIZE), page),
                  pl.BlockSpec((1, FOX_HEADS, PAGE_SIZE), page),
                  whole((FOX_HEADS, D_MODEL)), whole((PAGE_SIZE, PAGE_SIZE))],
        out_specs=per_seq(1, D_MODEL),
        scratch_shapes=[pltpu.VMEM((FOX_HEADS, D_MODEL), BF16),
                        pltpu.VMEM((FOX_HEADS, 1), F32), pltpu.VMEM((FOX_HEADS, 1), F32),
                        pltpu.VMEM((FOX_HEADS, D_MODEL), F32), pltpu.VMEM((FOX_HEADS, 1), F32)],
    )
    row = lambda a: a.astype(F32).reshape(nb, 1, -1)
    out = pl.pallas_call(
        _fox_decode_kernel,
        grid_spec=grid_spec,
        out_shape=jax.ShapeDtypeStruct((nb, 1, D_MODEL), BF16),
        compiler_params=_params("parallel", "arbitrary"),
        name="fox_decode",
    )(page_table, row(q), row(k_new), row(v_new), lf_new.reshape(nb, FOX_HEADS, 1),
      kt, vt, lft, jnp.asarray(_head_rows()), later)
    return out.reshape(nb, D_MODEL)


def _gla_proj_kernel(x_ref, w_ref, wa1_ref, wa2_ref, ba_ref, q_ref, k_ref, v_ref, r_ref, la_ref):
    xb = x_ref[...].astype(BF16)
    q_ref[...] = _dot(xb, w_ref[:, 0:GLA_DK])
    k_ref[...] = _dot(xb, w_ref[:, GLA_DK:2 * GLA_DK])
    v_ref[...] = _dot(xb, w_ref[:, 2 * GLA_DK:2 * GLA_DK + GLA_DV]).astype(BF16)
    r_ref[...] = _dot(xb, w_ref[:, 2 * GLA_DK + GLA_DV:2 * GLA_DK + 2 * GLA_DV])
    a_lr = _dot(xb, wa1_ref[...]).astype(BF16)
    la_ref[...] = _log_sigmoid(_dot(a_lr, wa2_ref[...]) + ba_ref[...]) / GLA_TAU


def _gla_proj(x, w, w_a1, w_a2, b_a, tm):
    n = x.shape[0]
    row = pl.BlockSpec((tm, D_MODEL), lambda i: (i, 0))
    half = pl.BlockSpec((tm, GLA_DK), lambda i: (i, 0))
    return pl.pallas_call(
        _gla_proj_kernel,
        grid=(n // tm,),
        in_specs=[row, _resident(w.shape), _resident(w_a1.shape), _resident(w_a2.shape), _resident((1, GLA_DK))],
        out_specs=[half, half, row, row, half],
        out_shape=[jax.ShapeDtypeStruct((n, GLA_DK), F32), jax.ShapeDtypeStruct((n, GLA_DK), F32),
                   jax.ShapeDtypeStruct((n, GLA_DV), BF16), jax.ShapeDtypeStruct((n, GLA_DV), F32),
                   jax.ShapeDtypeStruct((n, GLA_DK), F32)],
        compiler_params=_params("parallel"),
        name="gla_proj",
    )(x, w, w_a1, w_a2, b_a)


def _row_to_col(row):
    n = row.shape[1]
    eye = lax.broadcasted_iota(jnp.int32, (n, n), 0) == lax.broadcasted_iota(jnp.int32, (n, n), 1)
    return jnp.sum(jnp.where(eye, row, 0.0), axis=1, keepdims=True)


def _gla_chunk_kernel(q_ref, k_ref, v_ref, la_ref, tri_ref, o_ref, s_out_ref, s_ref, *, n_chunks):
    t = pl.program_id(2)

    @pl.when(t == 0)
    def _():
        s_ref[...] = jnp.zeros_like(s_ref)

    c = GLA_CHUNK
    tril = lax.broadcasted_iota(jnp.int32, (c, c), 1) <= lax.broadcasted_iota(jnp.int32, (c, c), 0)
    s = s_ref[...]
    for n in range(n_chunks):
        rows = slice(n * c, (n + 1) * c)
        bc = _tri_dot(tri_ref[...], la_ref[rows, :])
        b_last = bc[c - 1:c, :]
        k = k_ref[rows, :]
        v = v_ref[rows, :]
        q_dec = (q_ref[rows, :] * GLA_DK_H ** -0.5 * jnp.exp(bc)).astype(BF16)
        k_inv = (k * jnp.exp(-bc)).astype(BF16)
        k_end = (k * jnp.exp(b_last - bc)).astype(BF16)
        attn = jnp.where(tril, _dot_nt(q_dec, k_inv), 0.0).astype(BF16)
        o_ref[rows, :] = _dot(attn, v) + _dot(q_dec, s.astype(BF16))
        s = _row_to_col(jnp.exp(b_last)) * s + _dot_tn(k_end, v)
    s_ref[...] = s

    @pl.when(t == pl.num_programs(2) - 1)
    def _():
        s_out_ref[0, 0] = s


def _gla_chunked(q, k, v, la, batch, seq, ts):
    nt = seq // ts
    tri = jnp.asarray(np.tril(np.ones((GLA_CHUNK, GLA_CHUNK), np.float32)), BF16)
    key = pl.BlockSpec((ts, GLA_DK_H), lambda b, h, t: (b * nt + t, h))
    val = pl.BlockSpec((ts, GLA_DV_H), lambda b, h, t: (b * nt + t, h))
    return pl.pallas_call(
        functools.partial(_gla_chunk_kernel, n_chunks=ts // GLA_CHUNK),
        grid=(batch, GLA_HEADS, nt),
        in_specs=[key, key, val, key, _resident((GLA_CHUNK, GLA_CHUNK))],
        out_specs=[val, pl.BlockSpec((1, 1, GLA_DK_H, GLA_DV_H), lambda b, h, t: (b, h, 0, 0))],
        out_shape=[jax.ShapeDtypeStruct((batch * seq, GLA_DV), F32),
                   jax.ShapeDtypeStruct((batch, GLA_HEADS, GLA_DK_H, GLA_DV_H), F32)],
        scratch_shapes=[pltpu.VMEM((GLA_DK_H, GLA_DV_H), F32)],
        compiler_params=_params("parallel", "parallel", "arbitrary"),
        name="gla_chunked",
    )(q, k, v, la, tri)


def _gla_step_kernel(q_ref, k_ref, v_ref, la_ref, s0_ref, o_ref, s_ref):
    la = la_ref[0, 0]
    k = k_ref[0, 0]
    v = v_ref[0, 0]
    s0 = s0_ref[0, 0]
    decay = jnp.exp(la)
    q_dec = (q_ref[0, 0] * GLA_DK_H ** -0.5 * decay).astype(BF16)
    k_inv = (k * jnp.exp(-la)).astype(BF16)
    attn = jnp.sum(q_dec.astype(F32) * k_inv.astype(F32), axis=1, keepdims=True).astype(BF16).astype(F32)
    q_rows = jnp.broadcast_to(q_dec.astype(F32), (SUBLANES, GLA_DK_H)).astype(BF16)
    o_ref[0, 0] = attn * v + _dot(q_rows, s0.astype(BF16))[0:1, :]
    s_ref[0, 0] = _row_to_col(decay) * s0 + _row_to_col(k.astype(BF16).astype(F32)) * v


def _gla_step(q, k, v, la, s0):
    nb = q.shape[0]
    heads = lambda a, width: a.astype(F32).reshape(nb, GLA_HEADS, 1, width)
    key = pl.BlockSpec((1, 1, 1, GLA_DK_H), lambda b, h: (b, h, 0, 0))
    val = pl.BlockSpec((1, 1, 1, GLA_DV_H), lambda b, h: (b, h, 0, 0))
    state = pl.BlockSpec((1, 1, GLA_DK_H, GLA_DV_H), lambda b, h: (b, h, 0, 0))
    o, s = pl.pallas_call(
        _gla_step_kernel,
        grid=(nb, GLA_HEADS),
        in_specs=[key, key, val, key, state],
        out_specs=[val, state],
        out_shape=[jax.ShapeDtypeStruct((nb, GLA_HEADS, 1, GLA_DV_H), F32),
                   jax.ShapeDtypeStruct((nb, GLA_HEADS, GLA_DK_H, GLA_DV_H), F32)],
        compiler_params=_params("parallel", "parallel"),
        name="gla_step",
    )(heads(q, GLA_DK_H), heads(k, GLA_DK_H), heads(v, GLA_DV_H), heads(la, GLA_DK_H), s0)
    return o.reshape(nb, GLA_DV), s


PROMPT_TILE = 512
ATTN_TILE = 512
AUG_TILE = 512
GLA_TILE = 512


def kernel(x_prompt, x_sample, cache_fox_k, cache_fox_v, cache_fox_logf, state_gla, page_table,
           ln_g, ln_b, ffn_w_in, ffn_w_out, fox_w_in, fox_b_f, fox_w_o,
           gla_w_in, gla_w_a2, gla_b_a, gla_norm_g, gla_w_o):
    batch, seq, _ = x_prompt.shape
    nb = x_sample.shape[0]
    xp = x_prompt.reshape(batch * seq, D_MODEL)
    xs = x_sample.reshape(nb, D_MODEL)
    tp = min(PROMPT_TILE, batch * seq)
    vec = lambda a: a.reshape(1, -1)
    w_in = ffn_w_in.astype(BF16)
    w_out = ffn_w_out.astype(BF16)

    def ffn(i, half, ln):
        return (w_in[i, half], w_out[i, half], vec(ln_g[i, ln]), vec(ln_b[i, ln]))

    xp = _ffn_half(xp, *ffn(0, 0, 0), tp)
    xs = _ffn_half(xs, *ffn(0, 0, 0), nb)
    w_fox = fox_w_in[0].astype(BF16)
    w_qkv, w_f = w_fox[:, :3 * D_MODEL], w_fox[:, 3 * D_MODEL:]
    qp, kp, vp, kbp, vbp, lfp = _fox_proj(xp, w_qkv, w_f, vec(fox_b_f[0]), tp)
    qs, ks, vs, _, _, lfs = _fox_proj(xs, w_qkv, w_f, vec(fox_b_f[0]), nb)
    qa, ka = _fox_aug(qp, kbp, lfp, batch, seq, min(AUG_TILE, seq))
    mp = _fox_attn(qa, ka, vbp, batch, seq, min(ATTN_TILE, seq))
    ms = _fox_decode(qs, ks, vs, lfs, cache_fox_k[0], cache_fox_v[0], cache_fox_logf[0], page_table)
    mix = (fox_w_o[0].astype(BF16), vec(ln_g[0, 1]), vec(ln_b[0, 1])) + ffn(0, 1, 2)
    xp = _mix_ffn(xp, mp, *mix, tp)
    xs = _mix_ffn(xs, ms, *mix, nb)

    xp = _ffn_half(xp, *ffn(1, 0, 0), tp)
    xs = _ffn_half(xs, *ffn(1, 0, 0), nb)
    w_gla = gla_w_in[0].astype(BF16)
    n_main = 2 * GLA_DK + 2 * GLA_DV
    proj = (w_gla[:, :n_main], w_gla[:, n_main:], gla_w_a2[0].astype(BF16), vec(gla_b_a[0]))
    gqp, gkp, gvp, grp, glap = _gla_proj(xp, *proj, tp)
    gqs, gks, gvs, grs, glas = _gla_proj(xs, *proj, nb)
    op, state_p = _gla_chunked(gqp, gkp, gvp, glap, batch, seq, min(GLA_TILE, seq))
    os_, state_s = _gla_step(gqs, gks, gvs, glas, state_gla[0])
    mix = (vec(gla_norm_g[0]), gla_w_o[0].astype(BF16), vec(ln_g[1, 1]), vec(ln_b[1, 1])) + ffn(1, 1, 2)
    xp = _gla_mix_ffn(xp, op, grp, *mix, tp)
    xs = _gla_mix_ffn(xs, os_, grs, *mix, nb)

    heads = (FOX_HEADS, FOX_HEAD_DIM)
    return (xp.reshape(batch, seq, D_MODEL), xs.reshape(nb, 1, D_MODEL),
            kp.reshape(1, batch, seq, *heads), vp.reshape(1, batch, seq, *heads),
            lfp.reshape(1, batch, seq, FOX_HEADS),
            ks.reshape(1, nb, 1, *heads), vs.reshape(1, nb, 1, *heads), lfs.reshape(1, nb, 1, FOX_HEADS),
            state_p[None], state_s[None])
```

```python
import functools

import numpy as np
import jax
import jax.numpy as jnp
from jax import lax
from jax.experimental import pallas as pl
from jax.experimental.pallas import tpu as pltpu

F32 = jnp.float32
BF16 = jnp.bfloat16

D_MODEL = 1024
DEPTH = 2
PAGE_SIZE = 128
FOX_HEADS = 16
FOX_HEAD_DIM = D_MODEL // FOX_HEADS
FOX_PAIRS = FOX_HEADS // 2
GLA_HEADS = 4
GLA_DK = D_MODEL // 2
GLA_DV = D_MODEL
GLA_DK_H = GLA_DK // GLA_HEADS
GLA_DV_H = GLA_DV // GLA_HEADS
GLA_RANK = 16
GLA_TAU = 16.0
GLA_CHUNK = 64
D_FF = ((8 * D_MODEL // 3 + 127) // 128) * 128
DN_ALPHA = (2 * DEPTH) ** 0.25
LN_EPS = 1e-5
MASK_VALUE = -1e30
LOG2E = 1.4426950408889634

LANES = 128
SUBLANES = 8
MXU_WIDTH = 256
FF_CHUNK = MXU_WIDTH
AUG_PARTS = 3
VMEM_LIMIT = 56 * 2 ** 20
DECODE_GROUP = 8
ATTN_ONES_ROWS = 16


def _params(*sem):
    return pltpu.CompilerParams(dimension_semantics=sem, vmem_limit_bytes=VMEM_LIMIT)


def _resident(shape):
    return pl.BlockSpec(shape, lambda *_: (0,) * len(shape), pipeline_mode=pl.Buffered(1))


def _dot(a, b):
    return jnp.dot(a, b, preferred_element_type=F32)


def _dot_nt(a, b):
    return lax.dot_general(a, b, (((1,), (1,)), ((), ())), preferred_element_type=F32)


def _dot_tn(a, b):
    return lax.dot_general(a, b, (((0,), (0,)), ((), ())), preferred_element_type=F32)


def _layer_norm(y, g, b):
    mu = jnp.mean(y, axis=-1, keepdims=True)
    d = y - mu
    var = jnp.mean(d * d, axis=-1, keepdims=True)
    return d * lax.rsqrt(var + LN_EPS) * g + b


def _log_sigmoid(z):
    return jnp.minimum(z, 0.0) - jnp.log1p(jnp.exp(-jnp.abs(z)))


def _split3(x):
    hi = x.astype(BF16).astype(F32)
    r = x - hi
    mid = r.astype(BF16).astype(F32)
    lo = (r - mid).astype(BF16).astype(F32)
    return hi, mid, lo


def _tri_dot(tri, x):
    hi, mid, lo = _split3(x)
    return (_dot(tri, hi.astype(BF16)) + _dot(tri, mid.astype(BF16))) + _dot(tri, lo.astype(BF16))


def _dot_tri(x, tri):
    hi, mid, lo = _split3(x)
    return (_dot(hi.astype(BF16), tri) + _dot(mid.astype(BF16), tri)) + _dot(lo.astype(BF16), tri)


def _swiglu_ln(x, win_ref, wout_ref, g, b):
    xb = x.astype(BF16)
    acc = jnp.zeros(x.shape, F32)
    for c in range(D_FF // FF_CHUNK):
        lo = c * FF_CHUNK
        gate = _dot(xb, win_ref[:, lo:lo + FF_CHUNK])
        up = _dot(xb, win_ref[:, D_FF + lo:D_FF + lo + FF_CHUNK])
        h = (gate * jax.nn.sigmoid(gate) * up).astype(BF16)
        acc = acc + _dot(h, wout_ref[lo:lo + FF_CHUNK, :])
    return _layer_norm(DN_ALPHA * x + 0.5 * acc, g, b)


def _ffn_kernel(x_ref, win_ref, wout_ref, g_ref, b_ref, o_ref):
    o_ref[...] = _swiglu_ln(x_ref[...], win_ref, wout_ref, g_ref[...], b_ref[...])


def _ffn_half(x, w_in, w_out, g, b, tm):
    n = x.shape[0]
    row = pl.BlockSpec((tm, D_MODEL), lambda i: (i, 0))
    return pl.pallas_call(
        _ffn_kernel,
        grid=(n // tm,),
        in_specs=[row, _resident(w_in.shape), _resident(w_out.shape),
                  _resident((1, D_MODEL)), _resident((1, D_MODEL))],
        out_specs=row,
        out_shape=jax.ShapeDtypeStruct((n, D_MODEL), F32),
        compiler_params=_params("parallel"),
        name="ffn_half",
    )(x, w_in, w_out, g, b)


def _mix_ffn_kernel(x_ref, m_ref, wo_ref, g1_ref, b1_ref, win_ref, wout_ref, g2_ref, b2_ref, o_ref):
    x1 = _layer_norm(DN_ALPHA * x_ref[...] + _dot(m_ref[...], wo_ref[...]), g1_ref[...], b1_ref[...])
    o_ref[...] = _swiglu_ln(x1, win_ref, wout_ref, g2_ref[...], b2_ref[...])


def _gla_mix_ffn_kernel(x_ref, att_ref, r_ref, gn_ref, wo_ref, g1_ref, b1_ref,
                        win_ref, wout_ref, g2_ref, b2_ref, o_ref):
    heads = []
    for h in range(GLA_HEADS):
        o = att_ref[:, h * GLA_DV_H:(h + 1) * GLA_DV_H]
        mu = jnp.mean(o, axis=-1, keepdims=True)
        d = o - mu
        var = jnp.mean(d * d, axis=-1, keepdims=True)
        heads.append(d * lax.rsqrt(var + LN_EPS))
    on = jnp.concatenate(heads, axis=-1) * gn_ref[...]
    r = r_ref[...]
    m = (on * (r * jax.nn.sigmoid(r))).astype(BF16)
    x1 = _layer_norm(DN_ALPHA * x_ref[...] + _dot(m, wo_ref[...]), g1_ref[...], b1_ref[...])
    o_ref[...] = _swiglu_ln(x1, win_ref, wout_ref, g2_ref[...], b2_ref[...])


def _mix_ffn(x, m, w_o, g1, b1, w_in, w_out, g2, b2, tm):
    n = x.shape[0]
    row = pl.BlockSpec((tm, D_MODEL), lambda i: (i, 0))
    vec = _resident((1, D_MODEL))
    return pl.pallas_call(
        _mix_ffn_kernel,
        grid=(n // tm,),
        in_specs=[row, row, _resident(w_o.shape), vec, vec,
                  _resident(w_in.shape), _resident(w_out.shape), vec, vec],
        out_specs=row,
        out_shape=jax.ShapeDtypeStruct((n, D_MODEL), F32),
        compiler_params=_params("parallel"),
        name="mix_ffn",
    )(x, m, w_o, g1, b1, w_in, w_out, g2, b2)


def _gla_mix_ffn(x, o, r, gn, w_o, g1, b1, w_in, w_out, g2, b2, tm):
    n = x.shape[0]
    row = pl.BlockSpec((tm, D_MODEL), lambda i: (i, 0))
    vec = _resident((1, D_MODEL))
    return pl.pallas_call(
        _gla_mix_ffn_kernel,
        grid=(n // tm,),
        in_specs=[row, row, row, vec, _resident(w_o.shape), vec, vec,
                  _resident(w_in.shape), _resident(w_out.shape), vec, vec],
        out_specs=row,
        out_shape=jax.ShapeDtypeStruct((n, D_MODEL), F32),
        compiler_params=_params("parallel"),
        name="gla_mix_ffn",
    )(x, o, r, gn, w_o, g1, b1, w_in, w_out, g2, b2)


def _fox_proj_kernel(x_ref, wqkv_ref, wf_ref, bf_ref, q_ref, k_ref, v_ref, kb_ref, vb_ref, lf_ref):
    xb = x_ref[...].astype(BF16)
    q = _dot(xb, wqkv_ref[:, 0:D_MODEL])
    q_ref[...] = (q * FOX_HEAD_DIM ** -0.5).astype(BF16)
    k = _dot(xb, wqkv_ref[:, D_MODEL:2 * D_MODEL])
    k_ref[...] = k
    kb_ref[...] = k.astype(BF16)
    v = _dot(xb, wqkv_ref[:, 2 * D_MODEL:3 * D_MODEL])
    v_ref[...] = v
    vb_ref[...] = v.astype(BF16)
    lf_ref[...] = _log_sigmoid(_dot(xb, wf_ref[...]) + bf_ref[...])


def _fox_proj(x, w_qkv, w_f, b_f, tm):
    n = x.shape[0]
    row = pl.BlockSpec((tm, D_MODEL), lambda i: (i, 0))
    gate = pl.BlockSpec((tm, FOX_HEADS), lambda i: (i, 0))
    full = lambda dt: jax.ShapeDtypeStruct((n, D_MODEL), dt)
    return pl.pallas_call(
        _fox_proj_kernel,
        grid=(n // tm,),
        in_specs=[row, _resident(w_qkv.shape), _resident(w_f.shape), _resident((1, FOX_HEADS))],
        out_specs=[row, row, row, row, row, gate],
        out_shape=[full(BF16), full(F32), full(F32), full(BF16), full(BF16),
                   jax.ShapeDtypeStruct((n, FOX_HEADS), F32)],
        compiler_params=_params("parallel"),
        name="fox_proj",
    )(x, w_qkv, w_f, b_f)


def _fox_proj_t_kernel(x_ref, wt_ref, wk_ref, wf_ref, bf_ref, bft_ref,
                       qt_ref, kt_ref, kb_ref, vt_ref, vtb_ref, lf_ref, lft_ref):
    xb = x_ref[...].astype(BF16)
    q_t = _dot_nt(wt_ref[0:D_MODEL, :], xb)
    qt_ref[0] = (q_t * (FOX_HEAD_DIM ** -0.5 * LOG2E)).astype(BF16)
    kt_ref[0] = _dot_nt(wt_ref[D_MODEL:2 * D_MODEL, :], xb)
    kb_ref[...] = _dot(xb, wk_ref[...]).astype(BF16)
    v_t = _dot_nt(wt_ref[2 * D_MODEL:3 * D_MODEL, :], xb)
    vt_ref[0] = v_t
    vtb_ref[0] = v_t.astype(BF16)
    lf_ref[...] = _log_sigmoid(_dot(xb, wf_ref[...]) + bf_ref[...])
    lft_ref[0] = _log_sigmoid(_dot_nt(wt_ref[3 * D_MODEL:3 * D_MODEL + FOX_HEADS, :], xb) + bft_ref[...])


def _fox_proj_t(x, w_t, w_k, w_f, b_f, batch, seq, tm):
    nt = seq // tm
    row = pl.BlockSpec((tm, D_MODEL), lambda b, t: (b * nt + t, 0))
    gate = pl.BlockSpec((tm, FOX_HEADS), lambda b, t: (b * nt + t, 0))
    col = pl.BlockSpec((1, D_MODEL, tm), lambda b, t: (b, 0, t))
    gate_t = pl.BlockSpec((1, FOX_HEADS, tm), lambda b, t: (b, 0, t))
    feat = lambda dt: jax.ShapeDtypeStruct((batch, D_MODEL, seq), dt)
    return pl.pallas_call(
        _fox_proj_t_kernel,
        grid=(batch, nt),
        in_specs=[row, _resident(w_t.shape), _resident(w_k.shape), _resident(w_f.shape),
                  _resident((1, FOX_HEADS)), _resident((FOX_HEADS, 1))],
        out_specs=[col, col, row, col, col, gate, gate_t],
        out_shape=[feat(BF16), feat(F32), jax.ShapeDtypeStruct((batch * seq, D_MODEL), BF16), feat(F32), feat(BF16),
                   jax.ShapeDtypeStruct((batch * seq, FOX_HEADS), F32),
                   jax.ShapeDtypeStruct((batch, FOX_HEADS, seq), F32)],
        compiler_params=_params("parallel", "parallel"),
        name="fox_proj_t",
    )(x, w_t, w_k, w_f, b_f, b_f.reshape(FOX_HEADS, 1))


def _aug_placement():
    pq = np.zeros((AUG_PARTS, D_MODEL, FOX_HEADS), np.float32)
    pk = np.zeros((AUG_PARTS, FOX_HEADS, D_MODEL), np.float32)
    ones_q = np.zeros((D_MODEL, 1), np.float32)
    ones_k = np.zeros((1, D_MODEL), np.float32)
    for h in range(FOX_HEADS):
        base = (h // 2) * LANES + (FOX_HEAD_DIM if h % 2 == 0 else 0)
        for j in range(AUG_PARTS):
            pq[j, h * FOX_HEAD_DIM + AUG_PARTS + j, h] = 1.0
            ones_q[h * FOX_HEAD_DIM + j, 0] = 1.0
            pk[j, h, base + j] = -1.0
            ones_k[0, base + AUG_PARTS + j] = 1.0
    return pq, pk, ones_q, ones_k


def _fox_aug_kernel(qt_ref, k_ref, lf_ref, lft_ref, tril_ref, triu_ref, pq_ref, pk_ref, oq_ref, ok_ref,
                    qa_ref, ka_ref, carry_row_ref, carry_col_ref):
    @pl.when(pl.program_id(1) == 0)
    def _():
        carry_row_ref[...] = jnp.zeros_like(carry_row_ref)
        carry_col_ref[...] = jnp.zeros_like(carry_col_ref)

    tm = k_ref.shape[0]
    c = _tri_dot(tril_ref[...], lf_ref[...]) + carry_row_ref[...]
    c_t = _dot_tri(lft_ref[0], triu_ref[...]) + carry_col_ref[...]
    carry_row_ref[...] = c[tm - 1:tm, :]
    carry_col_ref[...] = c_t[:, tm - 1:tm]

    aug_k = ok_ref[...]
    for j, part in enumerate(_split3(c * LOG2E)):
        aug_k = aug_k + _dot(part.astype(BF16), pk_ref[j])
    lane = lax.broadcasted_iota(jnp.int32, (tm, D_MODEL), 1) & (LANES - 1)
    low = lane < FOX_HEAD_DIM
    k = k_ref[...].astype(F32)
    k_even = jnp.where(low, k, aug_k).astype(BF16)
    k_odd = jnp.where(low, aug_k, k).astype(BF16)
    for p in range(FOX_PAIRS):
        src = slice(p * LANES, (p + 1) * LANES)
        ka_ref[:, 2 * p * LANES:(2 * p + 1) * LANES] = k_even[:, src]
        ka_ref[:, (2 * p + 1) * LANES:(2 * p + 2) * LANES] = k_odd[:, src]

    aug_q = jnp.broadcast_to(oq_ref[...], (D_MODEL, tm))
    for j, part in enumerate(_split3(c_t * LOG2E)):
        aug_q = aug_q + _dot(pq_ref[j], part.astype(BF16))
    aug_q = aug_q.astype(BF16)
    for h in range(FOX_HEADS):
        feat = slice(h * FOX_HEAD_DIM, (h + 1) * FOX_HEAD_DIM)
        first = slice(h * LANES, h * LANES + FOX_HEAD_DIM)
        second = slice(h * LANES + FOX_HEAD_DIM, (h + 1) * LANES)
        q_rows, spare_rows = (first, second) if h % 2 == 0 else (second, first)
        qa_ref[0, q_rows, :] = qt_ref[0, feat, :]
        qa_ref[0, spare_rows, :] = aug_q[feat, :]


def _fox_aug(qt, kb, lf, lft, batch, seq, tm):
    pq, pk, ones_q, ones_k = _aug_placement()
    tril = jnp.asarray(np.tril(np.ones((tm, tm), np.float32)), BF16)
    triu = jnp.asarray(np.triu(np.ones((tm, tm), np.float32)), BF16)
    nt = seq // tm
    row = pl.BlockSpec((tm, D_MODEL), lambda b, t: (b * nt + t, 0))
    gate = pl.BlockSpec((tm, FOX_HEADS), lambda b, t: (b * nt + t, 0))
    wide = pl.BlockSpec((tm, 2 * D_MODEL), lambda b, t: (b * nt + t, 0))
    col = pl.BlockSpec((1, D_MODEL, tm), lambda b, t: (b, 0, t))
    gate_t = pl.BlockSpec((1, FOX_HEADS, tm), lambda b, t: (b, 0, t))
    tall = pl.BlockSpec((1, 2 * D_MODEL, tm), lambda b, t: (b, 0, t))
    return pl.pallas_call(
        _fox_aug_kernel,
        grid=(batch, nt),
        in_specs=[col, row, gate, gate_t, _resident((tm, tm)), _resident((tm, tm)),
                  _resident(pq.shape), _resident(pk.shape), _resident((D_MODEL, 1)), _resident((1, D_MODEL))],
        out_specs=[tall, wide],
        out_shape=[jax.ShapeDtypeStruct((batch, 2 * D_MODEL, seq), BF16),
                   jax.ShapeDtypeStruct((batch * seq, 2 * D_MODEL), BF16)],
        scratch_shapes=[pltpu.VMEM((1, FOX_HEADS), F32), pltpu.VMEM((FOX_HEADS, 1), F32)],
        compiler_params=_params("parallel", "arbitrary"),
        name="fox_aug",
    )(qt, kb, lf, lft, tril, triu, jnp.asarray(pq, BF16), jnp.asarray(pk, BF16),
      jnp.asarray(ones_q), jnp.asarray(ones_k))


def _fox_attn_kernel(qa_ref, ka_ref, v_ref, o_ref, s0_scr, s1_scr, p0_scr, p1_scr, acc_scr, *, tq, tk):
    i = pl.program_id(2)
    q_t = (qa_ref[0, 0:LANES, :], qa_ref[0, LANES:2 * LANES, :])
    hd = FOX_HEAD_DIM
    heads = range(2)
    s_scr = (s0_scr, s1_scr)
    p_scr = (p0_scr, p1_scr)
    ones_rows = jnp.ones((ATTN_ONES_ROWS, tk), BF16)

    def logits(b, slot):
        start = pl.multiple_of(b * tk, tk)
        for e in heads:
            s_scr[slot][e] = _dot(ka_ref[pl.ds(start, tk), e * LANES:(e + 1) * LANES], q_t[e])

    def softmax(e, slot, m_old, mask):
        s = s_scr[slot][e] if mask is None else jnp.where(mask, s_scr[slot][e], MASK_VALUE)
        m_new = jnp.maximum(m_old[e], jnp.max(s, axis=0, keepdims=True))
        p_scr[slot][e] = jnp.exp2(s - m_new).astype(BF16)
        return m_new, jnp.exp2(m_old[e] - m_new)

    def weighted_values(b, slot, corr):
        start = pl.multiple_of(b * tk, tk)
        for e in heads:
            v_ext = jnp.concatenate([v_ref[0, e * hd:(e + 1) * hd, pl.ds(start, tk)], ones_rows], axis=0)
            acc_scr[e] = corr[e] * acc_scr[e] + _dot(v_ext, p_scr[slot][e])

    def step(b, slot, m_old, corr_prev, mask, last):
        weighted_values(jnp.maximum(b - 1, 0), 1 - slot, corr_prev)
        first = softmax(0, slot, m_old, mask)
        if not last:
            logits(b + 1, 1 - slot)
        second = softmax(1, slot, m_old, mask)
        return tuple(zip(first, second))

    per_q = tq // tk

    def full_blocks(t, carry):
        for d in range(per_q):
            carry = step(per_q * t + d, d % 2, *carry, None, False)
        return carry

    logits(0, 0)
    p1_scr[...] = jnp.zeros_like(p1_scr)
    acc_scr[...] = jnp.zeros_like(acc_scr)
    carry = ((jnp.full((1, tq), MASK_VALUE, F32),) * 2, (jnp.ones((1, tq), F32),) * 2)
    carry = lax.fori_loop(0, i, full_blocks, carry)
    n_full = i * per_q
    key = lax.broadcasted_iota(jnp.int32, (tk, tq), 0)
    query = lax.broadcasted_iota(jnp.int32, (tk, tq), 1)
    for d in range(per_q):
        carry = step(n_full + d, d % 2, *carry, key + d * tk <= query, d == per_q - 1)
    weighted_values(n_full + per_q - 1, (per_q - 1) % 2, carry[1])
    out_t = jnp.concatenate([acc_scr[e, 0:hd, :] * (1.0 / acc_scr[e, hd:hd + 1, :]) for e in heads], axis=0)
    o_ref[...] = out_t.T.astype(BF16)


def _fox_attn(qa, ka, vtb, batch, seq, tq, tk):
    nq = seq // tq
    assert (tq // tk) % 2 == 0, "key blocks alternate between two scratch slots"
    return pl.pallas_call(
        functools.partial(_fox_attn_kernel, tq=tq, tk=tk),
        grid=(batch, FOX_PAIRS, nq),
        in_specs=[pl.BlockSpec((1, 2 * LANES, tq), lambda b, p, i: (b, p, i)),
                  pl.BlockSpec((seq, 2 * LANES), lambda b, p, i: (b, p)),
                  pl.BlockSpec((1, LANES, seq), lambda b, p, i: (b, p, 0))],
        out_specs=pl.BlockSpec((tq, LANES), lambda b, p, i: (b * nq + i, p)),
        out_shape=jax.ShapeDtypeStruct((batch * seq, D_MODEL), BF16),
        scratch_shapes=[pltpu.VMEM((2, tk, tq), F32), pltpu.VMEM((2, tk, tq), F32),
                        pltpu.VMEM((2, tk, tq), BF16), pltpu.VMEM((2, tk, tq), BF16),
                        pltpu.VMEM((2, FOX_HEAD_DIM + ATTN_ONES_ROWS, tq), F32)],
        compiler_params=_params("parallel", "parallel", "arbitrary"),
        name="fox_attn",
    )(qa, ka, vtb)


def _head_rows():
    sel = np.zeros((FOX_HEADS, D_MODEL), np.float32)
    for h in range(FOX_HEADS):
        sel[h, h * FOX_HEAD_DIM:(h + 1) * FOX_HEAD_DIM] = 1.0
    return sel


def _fox_decode_kernel(pt_ref, q_ref, kn_ref, vn_ref, lfn_ref, *refs, group):
    ck_refs, cv_refs, clf_refs = refs[0:group], refs[group:2 * group], refs[2 * group:3 * group]
    sel_ref, later_ref, o_ref, qrows_ref, m_ref, l_ref, acc_ref, suf_ref = refs[3 * group:]
    j = pl.program_id(1)

    @pl.when(j == 0)
    def _():
        qrows = (sel_ref[...] * q_ref[0]).astype(BF16)
        qrows_ref[...] = qrows
        kn = jnp.broadcast_to(kn_ref[0], (SUBLANES, D_MODEL)).astype(BF16)
        m_ref[...] = _dot_nt(qrows, kn)[:, 0:1]
        l_ref[...] = jnp.ones_like(l_ref)
        acc_ref[...] = jnp.broadcast_to(vn_ref[0].astype(BF16).astype(F32), acc_ref.shape)
        suf_ref[...] = lfn_ref[0]

    suf = suf_ref[...]
    logits = []
    for g in range(group):
        lf = clf_refs[g][0]
        bias = _dot_tri(lf, later_ref[...]) + suf
        logits.append(_dot(qrows_ref[...], ck_refs[g][0].astype(BF16)) + bias)
        suf = suf + jnp.sum(lf, axis=1, keepdims=True)
    suf_ref[...] = suf
    s = jnp.concatenate(logits, axis=1)
    m_old = m_ref[...]
    m_new = jnp.maximum(m_old, jnp.max(s, axis=1, keepdims=True))
    corr = jnp.exp(m_old - m_new)
    p = jnp.exp(s - m_new)
    l_ref[...] = corr * l_ref[...] + jnp.sum(p, axis=1, keepdims=True)
    m_ref[...] = m_new
    pb = p.astype(BF16)
    pv = _dot_nt(pb[:, 0:PAGE_SIZE], cv_refs[0][0].astype(BF16))
    for g in range(1, group):
        pv = pv + _dot_nt(pb[:, g * PAGE_SIZE:(g + 1) * PAGE_SIZE], cv_refs[g][0].astype(BF16))
    acc_ref[...] = corr * acc_ref[...] + pv

    @pl.when(j == pl.num_programs(1) - 1)
    def _():
        out = jnp.sum(sel_ref[...] * acc_ref[...] * (1.0 / l_ref[...]), axis=0, keepdims=True)
        o_ref[0] = out.astype(BF16)


def _fox_decode(q, k_new, v_new, lf_new, cache_k, cache_v, cache_lf, page_table):
    nb, n_pages = page_table.shape
    n_pool = cache_k.shape[0]
    later = jnp.asarray(np.tril(np.ones((PAGE_SIZE, PAGE_SIZE), np.float32), -1), BF16)
    kt = jnp.transpose(cache_k, (0, 2, 3, 1)).reshape(n_pool, D_MODEL, PAGE_SIZE)
    vt = jnp.transpose(cache_v, (0, 2, 3, 1)).reshape(n_pool, D_MODEL, PAGE_SIZE)
    lft = jnp.transpose(cache_lf, (0, 2, 1))
    group = DECODE_GROUP if n_pages % DECODE_GROUP == 0 else 1

    def page(g):
        return lambda b, j, pt: (pt[b, n_pages - 1 - (j * group + g)], 0, 0)

    def pages(rows):
        return [pl.BlockSpec((1, rows, PAGE_SIZE), page(g)) for g in range(group)]

    whole = lambda shape: pl.BlockSpec(shape, lambda b, j, pt: (0,) * len(shape))
    per_seq = lambda *shape: pl.BlockSpec((1,) + shape, lambda b, j, pt: (b, 0, 0))
    grid_spec = pltpu.PrefetchScalarGridSpec(
        num_scalar_prefetch=1,
        grid=(nb, n_pages // group),
        in_specs=[per_seq(1, D_MODEL), per_seq(1, D_MODEL), per_seq(1, D_MODEL), per_seq(FOX_HEADS, 1),
                  *pages(D_MODEL), *pages(D_MODEL), *pages(FOX_HEADS),
                  whole((FOX_HEADS, D_MODEL)), whole((PAGE_SIZE, PAGE_SIZE))],
        out_specs=per_seq(1, D_MODEL),
        scratch_shapes=[pltpu.VMEM((FOX_HEADS, D_MODEL), BF16),
                        pltpu.VMEM((FOX_HEADS, 1), F32), pltpu.VMEM((FOX_HEADS, 1), F32),
                        pltpu.VMEM((FOX_HEADS, D_MODEL), F32), pltpu.VMEM((FOX_HEADS, 1), F32)],
    )
    row = lambda a: a.astype(F32).reshape(nb, 1, -1)
    out = pl.pallas_call(
        functools.partial(_fox_decode_kernel, group=group),
        grid_spec=grid_spec,
        out_shape=jax.ShapeDtypeStruct((nb, 1, D_MODEL), BF16),
        compiler_params=_params("parallel", "arbitrary"),
        name="fox_decode",
    )(page_table, row(q), row(k_new), row(v_new), lf_new.reshape(nb, FOX_HEADS, 1),
      *([kt] * group), *([vt] * group), *([lft] * group), jnp.asarray(_head_rows()), later)
    return out.reshape(nb, D_MODEL)


def _gla_proj_kernel(x_ref, w_ref, wa1_ref, wa2_ref, ba_ref, q_ref, k_ref, v_ref, r_ref, la_ref):
    xb = x_ref[...].astype(BF16)
    q_ref[...] = _dot(xb, w_ref[:, 0:GLA_DK])
    k_ref[...] = _dot(xb, w_ref[:, GLA_DK:2 * GLA_DK])
    v_ref[...] = _dot(xb, w_ref[:, 2 * GLA_DK:2 * GLA_DK + GLA_DV]).astype(BF16)
    r_ref[...] = _dot(xb, w_ref[:, 2 * GLA_DK + GLA_DV:2 * GLA_DK + 2 * GLA_DV])
    a_lr = _dot(xb, wa1_ref[...]).astype(BF16)
    la_ref[...] = _log_sigmoid(_dot(a_lr, wa2_ref[...]) + ba_ref[...]) / GLA_TAU


def _gla_proj(x, w, w_a1, w_a2, b_a, tm):
    n = x.shape[0]
    row = pl.BlockSpec((tm, D_MODEL), lambda i: (i, 0))
    half = pl.BlockSpec((tm, GLA_DK), lambda i: (i, 0))
    return pl.pallas_call(
        _gla_proj_kernel,
        grid=(n // tm,),
        in_specs=[row, _resident(w.shape), _resident(w_a1.shape), _resident(w_a2.shape), _resident((1, GLA_DK))],
        out_specs=[half, half, row, row, half],
        out_shape=[jax.ShapeDtypeStruct((n, GLA_DK), F32), jax.ShapeDtypeStruct((n, GLA_DK), F32),
                   jax.ShapeDtypeStruct((n, GLA_DV), BF16), jax.ShapeDtypeStruct((n, GLA_DV), F32),
                   jax.ShapeDtypeStruct((n, GLA_DK), F32)],
        compiler_params=_params("parallel"),
        name="gla_proj",
    )(x, w, w_a1, w_a2, b_a)


def _row_to_col(row):
    n = row.shape[1]
    eye = lax.broadcasted_iota(jnp.int32, (n, n), 0) == lax.broadcasted_iota(jnp.int32, (n, n), 1)
    return jnp.sum(jnp.where(eye, row, 0.0), axis=1, keepdims=True)


def _gla_chunk_kernel(q_ref, k_ref, v_ref, la_ref, tri_ref, o_ref, s_out_ref, s_ref, *, n_chunks):
    t = pl.program_id(2)

    @pl.when(t == 0)
    def _():
        s_ref[...] = jnp.zeros_like(s_ref)

    c = GLA_CHUNK
    tril = lax.broadcasted_iota(jnp.int32, (c, c), 1) <= lax.broadcasted_iota(jnp.int32, (c, c), 0)
    s = s_ref[...]
    for n in range(n_chunks):
        rows = slice(n * c, (n + 1) * c)
        bc = _tri_dot(tri_ref[...], la_ref[rows, :])
        b_last = bc[c - 1:c, :]
        k = k_ref[rows, :]
        v = v_ref[rows, :]
        q_dec = (q_ref[rows, :] * GLA_DK_H ** -0.5 * jnp.exp(bc)).astype(BF16)
        k_inv = (k * jnp.exp(-bc)).astype(BF16)
        k_end = (k * jnp.exp(b_last - bc)).astype(BF16)
        attn = jnp.where(tril, _dot_nt(q_dec, k_inv), 0.0).astype(BF16)
        o_ref[rows, :] = _dot(attn, v) + _dot(q_dec, s.astype(BF16))
        s = _row_to_col(jnp.exp(b_last)) * s + _dot_tn(k_end, v)
    s_ref[...] = s

    @pl.when(t == pl.num_programs(2) - 1)
    def _():
        s_out_ref[0, 0] = s


def _gla_chunked(q, k, v, la, batch, seq, ts):
    nt = seq // ts
    tri = jnp.asarray(np.tril(np.ones((GLA_CHUNK, GLA_CHUNK), np.float32)), BF16)
    key = pl.BlockSpec((ts, GLA_DK_H), lambda b, h, t: (b * nt + t, h))
    val = pl.BlockSpec((ts, GLA_DV_H), lambda b, h, t: (b * nt + t, h))
    return pl.pallas_call(
        functools.partial(_gla_chunk_kernel, n_chunks=ts // GLA_CHUNK),
        grid=(batch, GLA_HEADS, nt),
        in_specs=[key, key, val, key, _resident((GLA_CHUNK, GLA_CHUNK))],
        out_specs=[val, pl.BlockSpec((1, 1, GLA_DK_H, GLA_DV_H), lambda b, h, t: (b, h, 0, 0))],
        out_shape=[jax.ShapeDtypeStruct((batch * seq, GLA_DV), F32),
                   jax.ShapeDtypeStruct((batch, GLA_HEADS, GLA_DK_H, GLA_DV_H), F32)],
        scratch_shapes=[pltpu.VMEM((GLA_DK_H, GLA_DV_H), F32)],
        compiler_params=_params("parallel", "parallel", "arbitrary"),
        name="gla_chunked",
    )(q, k, v, la, tri)


def _gla_step_kernel(q_ref, k_ref, v_ref, la_ref, s0_ref, o_ref, s_ref):
    la = la_ref[0, 0]
    k = k_ref[0, 0]
    v = v_ref[0, 0]
    s0 = s0_ref[0, 0]
    decay = jnp.exp(la)
    q_dec = (q_ref[0, 0] * GLA_DK_H ** -0.5 * decay).astype(BF16)
    k_inv = (k * jnp.exp(-la)).astype(BF16)
    attn = jnp.sum(q_dec.astype(F32) * k_inv.astype(F32), axis=1, keepdims=True).astype(BF16).astype(F32)
    q_rows = jnp.broadcast_to(q_dec.astype(F32), (SUBLANES, GLA_DK_H)).astype(BF16)
    o_ref[0, 0] = attn * v + _dot(q_rows, s0.astype(BF16))[0:1, :]
    s_ref[0, 0] = _row_to_col(decay) * s0 + _row_to_col(k.astype(BF16).astype(F32)) * v


def _gla_step(q, k, v, la, s0):
    nb = q.shape[0]
    heads = lambda a, width: a.astype(F32).reshape(nb, GLA_HEADS, 1, width)
    key = pl.BlockSpec((1, 1, 1, GLA_DK_H), lambda b, h: (b, h, 0, 0))
    val = pl.BlockSpec((1, 1, 1, GLA_DV_H), lambda b, h: (b, h, 0, 0))
    state = pl.BlockSpec((1, 1, GLA_DK_H, GLA_DV_H), lambda b, h: (b, h, 0, 0))
    o, s = pl.pallas_call(
        _gla_step_kernel,
        grid=(nb, GLA_HEADS),
        in_specs=[key, key, val, key, state],
        out_specs=[val, state],
        out_shape=[jax.ShapeDtypeStruct((nb, GLA_HEADS, 1, GLA_DV_H), F32),
                   jax.ShapeDtypeStruct((nb, GLA_HEADS, GLA_DK_H, GLA_DV_H), F32)],
        compiler_params=_params("parallel", "parallel"),
        name="gla_step",
    )(heads(q, GLA_DK_H), heads(k, GLA_DK_H), heads(v, GLA_DV_H), heads(la, GLA_DK_H), s0)
    return o.reshape(nb, GLA_DV), s


PROMPT_TILE = 512
ATTN_Q_TILE = 512
ATTN_K_TILE = 256
AUG_TILE = 512
GLA_TILE = 512


def kernel(x_prompt, x_sample, cache_fox_k, cache_fox_v, cache_fox_logf, state_gla, page_table,
           ln_g, ln_b, ffn_w_in, ffn_w_out, fox_w_in, fox_b_f, fox_w_o,
           gla_w_in, gla_w_a2, gla_b_a, gla_norm_g, gla_w_o):
    batch, seq, _ = x_prompt.shape
    nb = x_sample.shape[0]
    xp = x_prompt.reshape(batch * seq, D_MODEL)
    xs = x_sample.reshape(nb, D_MODEL)
    tp = min(PROMPT_TILE, batch * seq)
    vec = lambda a: a.reshape(1, -1)
    w_in = ffn_w_in.astype(BF16)
    w_out = ffn_w_out.astype(BF16)

    def ffn(i, half, ln):
        return (w_in[i, half], w_out[i, half], vec(ln_g[i, ln]), vec(ln_b[i, ln]))

    xp = _ffn_half(xp, *ffn(0, 0, 0), tp)
    xs = _ffn_half(xs, *ffn(0, 0, 0), nb)
    w_fox = fox_w_in[0].astype(BF16)
    w_qkv, w_f = w_fox[:, :3 * D_MODEL], w_fox[:, 3 * D_MODEL:]
    qt, kt, kbp, vt, vtb, lfp, lft = _fox_proj_t(xp, w_fox.T, w_fox[:, D_MODEL:2 * D_MODEL], w_f,
                                                 vec(fox_b_f[0]), batch, seq, min(PROMPT_TILE, seq))
    qs, ks, vs, _, _, lfs = _fox_proj(xs, w_qkv, w_f, vec(fox_b_f[0]), nb)
    qa, ka = _fox_aug(qt, kbp, lfp, lft, batch, seq, min(AUG_TILE, seq))
    mp = _fox_attn(qa, ka, vtb, batch, seq, min(ATTN_Q_TILE, seq), min(ATTN_K_TILE, seq))
    ms = _fox_decode(qs, ks, vs, lfs, cache_fox_k[0], cache_fox_v[0], cache_fox_logf[0], page_table)
    mix = (fox_w_o[0].astype(BF16), vec(ln_g[0, 1]), vec(ln_b[0, 1])) + ffn(0, 1, 2)
    xp = _mix_ffn(xp, mp, *mix, tp)
    xs = _mix_ffn(xs, ms, *mix, nb)

    xp = _ffn_half(xp, *ffn(1, 0, 0), tp)
    xs = _ffn_half(xs, *ffn(1, 0, 0), nb)
    w_gla = gla_w_in[0].astype(BF16)
    n_main = 2 * GLA_DK + 2 * GLA_DV
    proj = (w_gla[:, :n_main], w_gla[:, n_main:], gla_w_a2[0].astype(BF16), vec(gla_b_a[0]))
    gqp, gkp, gvp, grp, glap = _gla_proj(xp, *proj, tp)
    gqs, gks, gvs, grs, glas = _gla_proj(xs, *proj, nb)
    op, state_p = _gla_chunked(gqp, gkp, gvp, glap, batch, seq, min(GLA_TILE, seq))
    os_, state_s = _gla_step(gqs, gks, gvs, glas, state_gla[0])
    mix = (vec(gla_norm_g[0]), gla_w_o[0].astype(BF16), vec(ln_g[1, 1]), vec(ln_b[1, 1])) + ffn(1, 1, 2)
    xp = _gla_mix_ffn(xp, op, grp, *mix, tp)
    xs = _gla_mix_ffn(xs, os_, grs, *mix, nb)

    heads = (FOX_HEADS, FOX_HEAD_DIM)
    token_major = lambda a: jnp.transpose(a.reshape(batch, *heads, seq), (0, 3, 1, 2))[None]
    return (xp.reshape(batch, seq, D_MODEL), xs.reshape(nb, 1, D_MODEL),
            token_major(kt), token_major(vt), jnp.transpose(lft, (0, 2, 1))[None],
            ks.reshape(1, nb, 1, *heads), vs.reshape(1, nb, 1, *heads), lfs.reshape(1, nb, 1, FOX_HEADS),
            state_p[None], state_s[None])
```

```python
import functools

import numpy as np
import jax
import jax.numpy as jnp
from jax import lax
from jax.experimental import pallas as pl
from jax.experimental.pallas import tpu as pltpu

F32 = jnp.float32
BF16 = jnp.bfloat16

D_MODEL = 1024
DEPTH = 2
PAGE_SIZE = 128
FOX_HEADS = 16
FOX_HEAD_DIM = D_MODEL // FOX_HEADS
FOX_PAIRS = FOX_HEADS // 2
GLA_HEADS = 4
GLA_DK = D_MODEL // 2
GLA_DV = D_MODEL
GLA_DK_H = GLA_DK // GLA_HEADS
GLA_DV_H = GLA_DV // GLA_HEADS
GLA_RANK = 16
GLA_TAU = 16.0
GLA_CHUNK = 64
D_FF = ((8 * D_MODEL // 3 + 127) // 128) * 128
DN_ALPHA = (2 * DEPTH) ** 0.25
LN_EPS = 1e-5
MASK_VALUE = -1e30
LOG2E = 1.4426950408889634

LANES = 128
SUBLANES = 8
MXU_WIDTH = 256
FF_CHUNK = MXU_WIDTH
AUG_PARTS = 3
VMEM_LIMIT = 56 * 2 ** 20
DECODE_GROUP = 8
ATTN_ONES_ROWS = 16


def _params(*sem):
    return pltpu.CompilerParams(dimension_semantics=sem, vmem_limit_bytes=VMEM_LIMIT)


def _resident(shape):
    return pl.BlockSpec(shape, lambda *_: (0,) * len(shape), pipeline_mode=pl.Buffered(1))


def _stacked_spec(stacked):
    w, index = stacked
    tail = w.shape[len(index):]
    return pl.BlockSpec((None,) * len(index) + tail, lambda *_: index + (0,) * len(tail),
                        pipeline_mode=pl.Buffered(1))


def _dot(a, b):
    return jnp.dot(a, b, preferred_element_type=F32)


def _dot_nt(a, b):
    return lax.dot_general(a, b, (((1,), (1,)), ((), ())), preferred_element_type=F32)


def _dot_tn(a, b):
    return lax.dot_general(a, b, (((0,), (0,)), ((), ())), preferred_element_type=F32)


def _layer_norm(y, g, b):
    mu = jnp.mean(y, axis=-1, keepdims=True)
    d = y - mu
    var = jnp.mean(d * d, axis=-1, keepdims=True)
    return d * lax.rsqrt(var + LN_EPS) * g + b


def _log_sigmoid(z):
    return jnp.minimum(z, 0.0) - jnp.log1p(jnp.exp(-jnp.abs(z)))


def _split3(x):
    hi = x.astype(BF16).astype(F32)
    r = x - hi
    mid = r.astype(BF16).astype(F32)
    lo = (r - mid).astype(BF16).astype(F32)
    return hi, mid, lo


def _tri_dot(tri, x):
    hi, mid, lo = _split3(x)
    return (_dot(tri, hi.astype(BF16)) + _dot(tri, mid.astype(BF16))) + _dot(tri, lo.astype(BF16))


def _dot_tri(x, tri):
    hi, mid, lo = _split3(x)
    return (_dot(hi.astype(BF16), tri) + _dot(mid.astype(BF16), tri)) + _dot(lo.astype(BF16), tri)


def _swiglu_ln(x, win_ref, wout_ref, g, b):
    xb = x.astype(BF16)
    acc = jnp.zeros(x.shape, F32)
    for c in range(D_FF // FF_CHUNK):
        lo = c * FF_CHUNK
        gate = _dot(xb, win_ref[:, lo:lo + FF_CHUNK])
        up = _dot(xb, win_ref[:, D_FF + lo:D_FF + lo + FF_CHUNK])
        h = (gate * jax.nn.sigmoid(gate) * up).astype(BF16)
        acc = acc + _dot(h, wout_ref[lo:lo + FF_CHUNK, :])
    return _layer_norm(DN_ALPHA * x + 0.5 * acc, g, b)


def _ffn_kernel(x_ref, win_ref, wout_ref, g_ref, b_ref, o_ref):
    o_ref[...] = _swiglu_ln(x_ref[...], win_ref, wout_ref, g_ref[...], b_ref[...])


def _ffn_half(x, w_in, w_out, g, b, tm):
    n = x.shape[0]
    row = pl.BlockSpec((tm, D_MODEL), lambda i: (i, 0))
    return pl.pallas_call(
        _ffn_kernel,
        grid=(n // tm,),
        in_specs=[row, _stacked_spec(w_in), _stacked_spec(w_out),
                  _resident((1, D_MODEL)), _resident((1, D_MODEL))],
        out_specs=row,
        out_shape=jax.ShapeDtypeStruct((n, D_MODEL), F32),
        compiler_params=_params("parallel"),
        name="ffn_half",
    )(x, w_in[0], w_out[0], g, b)


def _mix_ffn_kernel(x_ref, m_ref, wo_ref, g1_ref, b1_ref, win_ref, wout_ref, g2_ref, b2_ref, o_ref):
    x1 = _layer_norm(DN_ALPHA * x_ref[...] + _dot(m_ref[...], wo_ref[...]), g1_ref[...], b1_ref[...])
    o_ref[...] = _swiglu_ln(x1, win_ref, wout_ref, g2_ref[...], b2_ref[...])


def _gla_mix_ffn_kernel(x_ref, att_ref, r_ref, gn_ref, wo_ref, g1_ref, b1_ref,
                        win_ref, wout_ref, g2_ref, b2_ref, o_ref):
    heads = []
    for h in range(GLA_HEADS):
        o = att_ref[:, h * GLA_DV_H:(h + 1) * GLA_DV_H]
        mu = jnp.mean(o, axis=-1, keepdims=True)
        d = o - mu
        var = jnp.mean(d * d, axis=-1, keepdims=True)
        heads.append(d * lax.rsqrt(var + LN_EPS))
    on = jnp.concatenate(heads, axis=-1) * gn_ref[...]
    r = r_ref[...]
    m = (on * (r * jax.nn.sigmoid(r))).astype(BF16)
    x1 = _layer_norm(DN_ALPHA * x_ref[...] + _dot(m, wo_ref[...]), g1_ref[...], b1_ref[...])
    o_ref[...] = _swiglu_ln(x1, win_ref, wout_ref, g2_ref[...], b2_ref[...])


def _mix_ffn(x, m, w_o, g1, b1, w_in, w_out, g2, b2, tm):
    n = x.shape[0]
    row = pl.BlockSpec((tm, D_MODEL), lambda i: (i, 0))
    vec = _resident((1, D_MODEL))
    return pl.pallas_call(
        _mix_ffn_kernel,
        grid=(n // tm,),
        in_specs=[row, row, _resident(w_o.shape), vec, vec,
                  _stacked_spec(w_in), _stacked_spec(w_out), vec, vec],
        out_specs=row,
        out_shape=jax.ShapeDtypeStruct((n, D_MODEL), F32),
        compiler_params=_params("parallel"),
        name="mix_ffn",
    )(x, m, w_o, g1, b1, w_in[0], w_out[0], g2, b2)


def _gla_mix_ffn(x, o, r, gn, w_o, g1, b1, w_in, w_out, g2, b2, tm):
    n = x.shape[0]
    row = pl.BlockSpec((tm, D_MODEL), lambda i: (i, 0))
    vec = _resident((1, D_MODEL))
    return pl.pallas_call(
        _gla_mix_ffn_kernel,
        grid=(n // tm,),
        in_specs=[row, row, row, vec, _resident(w_o.shape), vec, vec,
                  _stacked_spec(w_in), _stacked_spec(w_out), vec, vec],
        out_specs=row,
        out_shape=jax.ShapeDtypeStruct((n, D_MODEL), F32),
        compiler_params=_params("parallel"),
        name="gla_mix_ffn",
    )(x, o, r, gn, w_o, g1, b1, w_in[0], w_out[0], g2, b2)


def _fox_proj_kernel(x_ref, wqkv_ref, wf_ref, bf_ref, q_ref, k_ref, v_ref, kb_ref, vb_ref, lf_ref):
    xb = x_ref[...].astype(BF16)
    q = _dot(xb, wqkv_ref[:, 0:D_MODEL])
    q_ref[...] = (q * FOX_HEAD_DIM ** -0.5).astype(BF16)
    k = _dot(xb, wqkv_ref[:, D_MODEL:2 * D_MODEL])
    k_ref[...] = k
    kb_ref[...] = k.astype(BF16)
    v = _dot(xb, wqkv_ref[:, 2 * D_MODEL:3 * D_MODEL])
    v_ref[...] = v
    vb_ref[...] = v.astype(BF16)
    lf_ref[...] = _log_sigmoid(_dot(xb, wf_ref[...]) + bf_ref[...])


def _fox_proj(x, w_qkv, w_f, b_f, tm):
    n = x.shape[0]
    row = pl.BlockSpec((tm, D_MODEL), lambda i: (i, 0))
    gate = pl.BlockSpec((tm, FOX_HEADS), lambda i: (i, 0))
    full = lambda dt: jax.ShapeDtypeStruct((n, D_MODEL), dt)
    return pl.pallas_call(
        _fox_proj_kernel,
        grid=(n // tm,),
        in_specs=[row, _resident(w_qkv.shape), _resident(w_f.shape), _resident((1, FOX_HEADS))],
        out_specs=[row, row, row, row, row, gate],
        out_shape=[full(BF16), full(F32), full(F32), full(BF16), full(BF16),
                   jax.ShapeDtypeStruct((n, FOX_HEADS), F32)],
        compiler_params=_params("parallel"),
        name="fox_proj",
    )(x, w_qkv, w_f, b_f)


def _fox_proj_t_kernel(x_ref, wt_ref, wk_ref, wf_ref, bf_ref, bft_ref,
                       qt_ref, kt_ref, kb_ref, vt_ref, vtb_ref, lf_ref, lft_ref):
    xb = x_ref[...].astype(BF16)
    q_t = _dot_nt(wt_ref[0:D_MODEL, :], xb)
    qt_ref[0] = (q_t * (FOX_HEAD_DIM ** -0.5 * LOG2E)).astype(BF16)
    kt_ref[0] = _dot_nt(wt_ref[D_MODEL:2 * D_MODEL, :], xb)
    kb_ref[...] = _dot(xb, wk_ref[...]).astype(BF16)
    v_t = _dot_nt(wt_ref[2 * D_MODEL:3 * D_MODEL, :], xb)
    vt_ref[0] = v_t
    vtb_ref[0] = v_t.astype(BF16)
    lf_ref[...] = _log_sigmoid(_dot(xb, wf_ref[...]) + bf_ref[...])
    lft_ref[0] = _log_sigmoid(_dot_nt(wt_ref[3 * D_MODEL:3 * D_MODEL + FOX_HEADS, :], xb) + bft_ref[...])


def _fox_proj_t(x, w_t, w_k, w_f, b_f, batch, seq, tm):
    nt = seq // tm
    row = pl.BlockSpec((tm, D_MODEL), lambda b, t: (b * nt + t, 0))
    gate = pl.BlockSpec((tm, FOX_HEADS), lambda b, t: (b * nt + t, 0))
    col = pl.BlockSpec((1, D_MODEL, tm), lambda b, t: (b, 0, t))
    gate_t = pl.BlockSpec((1, FOX_HEADS, tm), lambda b, t: (b, 0, t))
    feat = lambda dt: jax.ShapeDtypeStruct((batch, D_MODEL, seq), dt)
    return pl.pallas_call(
        _fox_proj_t_kernel,
        grid=(batch, nt),
        in_specs=[row, _resident(w_t.shape), _resident(w_k.shape), _resident(w_f.shape),
                  _resident((1, FOX_HEADS)), _resident((FOX_HEADS, 1))],
        out_specs=[col, col, row, col, col, gate, gate_t],
        out_shape=[feat(BF16), feat(F32), jax.ShapeDtypeStruct((batch * seq, D_MODEL), BF16), feat(F32), feat(BF16),
                   jax.ShapeDtypeStruct((batch * seq, FOX_HEADS), F32),
                   jax.ShapeDtypeStruct((batch, FOX_HEADS, seq), F32)],
        compiler_params=_params("parallel", "parallel"),
        name="fox_proj_t",
    )(x, w_t, w_k, w_f, b_f, b_f.reshape(FOX_HEADS, 1))


def _aug_placement():
    pk = np.zeros((AUG_PARTS, FOX_HEADS, D_MODEL), np.float32)
    ones_k = np.zeros((1, D_MODEL), np.float32)
    for h in range(FOX_HEADS):
        base = (h // 2) * LANES + (FOX_HEAD_DIM if h % 2 == 0 else 0)
        for j in range(AUG_PARTS):
            pk[j, h, base + j] = -1.0
            ones_k[0, base + AUG_PARTS + j] = 1.0
    return pk, ones_k


def _fox_aug_kernel(qt_ref, k_ref, lf_ref, lft_ref, tril_ref, triu_ref, pk_ref, ok_ref,
                    qa_ref, ka_ref, carry_row_ref, carry_col_ref):
    @pl.when(pl.program_id(1) == 0)
    def _():
        carry_row_ref[...] = jnp.zeros_like(carry_row_ref)
        carry_col_ref[...] = jnp.zeros_like(carry_col_ref)

    tm = k_ref.shape[0]
    c = _tri_dot(tril_ref[...], lf_ref[...]) + carry_row_ref[...]
    c_t = _dot_tri(lft_ref[0], triu_ref[...]) + carry_col_ref[...]
    carry_row_ref[...] = c[tm - 1:tm, :]
    carry_col_ref[...] = c_t[:, tm - 1:tm]

    aug_k = ok_ref[...]
    for j, part in enumerate(_split3(c * LOG2E)):
        aug_k = aug_k + _dot(part.astype(BF16), pk_ref[j])
    lane = lax.broadcasted_iota(jnp.int32, (tm, D_MODEL), 1) & (LANES - 1)
    low = lane < FOX_HEAD_DIM
    k = k_ref[...].astype(F32)
    k_even = jnp.where(low, k, aug_k).astype(BF16)
    k_odd = jnp.where(low, aug_k, k).astype(BF16)
    for p in range(FOX_PAIRS):
        src = slice(p * LANES, (p + 1) * LANES)
        ka_ref[:, 2 * p * LANES:(2 * p + 1) * LANES] = k_even[:, src]
        ka_ref[:, (2 * p + 1) * LANES:(2 * p + 2) * LANES] = k_odd[:, src]

    parts = _split3(c_t * LOG2E)
    group = 2 * SUBLANES
    r = lax.broadcasted_iota(jnp.int32, (group, tm), 0)
    padding = jnp.zeros((FOX_HEAD_DIM - group, tm), BF16)
    for h in range(FOX_HEADS):
        spare = jnp.where(r < AUG_PARTS, 1.0, 0.0)
        for j, part in enumerate(parts):
            spare = jnp.where(r == AUG_PARTS + j, part[h:h + 1, :], spare)
        feat = slice(h * FOX_HEAD_DIM, (h + 1) * FOX_HEAD_DIM)
        first = slice(h * LANES, h * LANES + FOX_HEAD_DIM)
        second = slice(h * LANES + FOX_HEAD_DIM, (h + 1) * LANES)
        q_rows, spare_rows = (first, second) if h % 2 == 0 else (second, first)
        qa_ref[0, q_rows, :] = qt_ref[0, feat, :]
        qa_ref[0, spare_rows, :] = jnp.concatenate([spare.astype(BF16), padding], axis=0)


def _fox_aug(qt, kb, lf, lft, batch, seq, tm):
    pk, ones_k = _aug_placement()
    tril = jnp.asarray(np.tril(np.ones((tm, tm), np.float32)), BF16)
    triu = jnp.asarray(np.triu(np.ones((tm, tm), np.float32)), BF16)
    nt = seq // tm
    row = pl.BlockSpec((tm, D_MODEL), lambda b, t: (b * nt + t, 0))
    gate = pl.BlockSpec((tm, FOX_HEADS), lambda b, t: (b * nt + t, 0))
    wide = pl.BlockSpec((tm, 2 * D_MODEL), lambda b, t: (b * nt + t, 0))
    col = pl.BlockSpec((1, D_MODEL, tm), lambda b, t: (b, 0, t))
    gate_t = pl.BlockSpec((1, FOX_HEADS, tm), lambda b, t: (b, 0, t))
    tall = pl.BlockSpec((1, 2 * D_MODEL, tm), lambda b, t: (b, 0, t))
    return pl.pallas_call(
        _fox_aug_kernel,
        grid=(batch, nt),
        in_specs=[col, row, gate, gate_t, _resident((tm, tm)), _resident((tm, tm)),
                  _resident(pk.shape), _resident((1, D_MODEL))],
        out_specs=[tall, wide],
        out_shape=[jax.ShapeDtypeStruct((batch, 2 * D_MODEL, seq), BF16),
                   jax.ShapeDtypeStruct((batch * seq, 2 * D_MODEL), BF16)],
        scratch_shapes=[pltpu.VMEM((1, FOX_HEADS), F32), pltpu.VMEM((FOX_HEADS, 1), F32)],
        compiler_params=_params("parallel", "arbitrary"),
        name="fox_aug",
    )(qt, kb, lf, lft, tril, triu, jnp.asarray(pk, BF16), jnp.asarray(ones_k))


def _fox_attn_kernel(qa_ref, ka_ref, v_ref, o_ref, s0_scr, s1_scr, p0_scr, p1_scr, acc_scr, *, tq, tk):
    i = pl.program_id(2)
    q_t = (qa_ref[0, 0:LANES, :], qa_ref[0, LANES:2 * LANES, :])
    hd = FOX_HEAD_DIM
    heads = range(2)
    s_scr = (s0_scr, s1_scr)
    p_scr = (p0_scr, p1_scr)
    ones_rows = jnp.ones((ATTN_ONES_ROWS, tk), BF16)

    def logits(b, slot):
        start = pl.multiple_of(b * tk, tk)
        for e in heads:
            s_scr[slot][e] = _dot(ka_ref[pl.ds(start, tk), e * LANES:(e + 1) * LANES], q_t[e])

    def softmax(e, slot, m_old, mask):
        s = s_scr[slot][e] if mask is None else jnp.where(mask, s_scr[slot][e], MASK_VALUE)
        m_new = jnp.maximum(m_old[e], jnp.max(s, axis=0, keepdims=True))
        p_scr[slot][e] = jnp.exp2(s - m_new).astype(BF16)
        return m_new, jnp.exp2(m_old[e] - m_new)

    def weighted_values(b, slot, corr):
        start = pl.multiple_of(b * tk, tk)
        for e in heads:
            v_ext = jnp.concatenate([v_ref[0, e * hd:(e + 1) * hd, pl.ds(start, tk)], ones_rows], axis=0)
            acc_scr[e] = corr[e] * acc_scr[e] + _dot(v_ext, p_scr[slot][e])

    def step(b, slot, m_old, corr_prev, mask, last):
        weighted_values(jnp.maximum(b - 1, 0), 1 - slot, corr_prev)
        first = softmax(0, slot, m_old, mask)
        if not last:
            logits(b + 1, 1 - slot)
        second = softmax(1, slot, m_old, mask)
        return tuple(zip(first, second))

    per_q = tq // tk

    def full_blocks(t, carry):
        for d in range(per_q):
            carry = step(per_q * t + d, d % 2, *carry, None, False)
        return carry

    logits(0, 0)
    p1_scr[...] = jnp.zeros_like(p1_scr)
    acc_scr[...] = jnp.zeros_like(acc_scr)
    carry = ((jnp.full((1, tq), MASK_VALUE, F32),) * 2, (jnp.ones((1, tq), F32),) * 2)
    carry = lax.fori_loop(0, i, full_blocks, carry)
    n_full = i * per_q
    key = lax.broadcasted_iota(jnp.int32, (tk, tq), 0)
    query = lax.broadcasted_iota(jnp.int32, (tk, tq), 1)
    for d in range(per_q):
        carry = step(n_full + d, d % 2, *carry, key + d * tk <= query, d == per_q - 1)
    weighted_values(n_full + per_q - 1, (per_q - 1) % 2, carry[1])
    out_t = jnp.concatenate([acc_scr[e, 0:hd, :] * (1.0 / acc_scr[e, hd:hd + 1, :]) for e in heads], axis=0)
    o_ref[...] = out_t.T.astype(BF16)


def _fox_attn(qa, ka, vtb, batch, seq, tq, tk):
    nq = seq // tq
    assert (tq // tk) % 2 == 0, "key blocks alternate between two scratch slots"
    return pl.pallas_call(
        functools.partial(_fox_attn_kernel, tq=tq, tk=tk),
        grid=(batch, FOX_PAIRS, nq),
        in_specs=[pl.BlockSpec((1, 2 * LANES, tq), lambda b, p, i: (b, p, i)),
                  pl.BlockSpec((seq, 2 * LANES), lambda b, p, i: (b, p)),
                  pl.BlockSpec((1, LANES, seq), lambda b, p, i: (b, p, 0))],
        out_specs=pl.BlockSpec((tq, LANES), lambda b, p, i: (b * nq + i, p)),
        out_shape=jax.ShapeDtypeStruct((batch * seq, D_MODEL), BF16),
        scratch_shapes=[pltpu.VMEM((2, tk, tq), F32), pltpu.VMEM((2, tk, tq), F32),
                        pltpu.VMEM((2, tk, tq), BF16), pltpu.VMEM((2, tk, tq), BF16),
                        pltpu.VMEM((2, FOX_HEAD_DIM + ATTN_ONES_ROWS, tq), F32)],
        compiler_params=_params("parallel", "parallel", "arbitrary"),
        name="fox_attn",
    )(qa, ka, vtb)


def _head_rows():
    sel = np.zeros((FOX_HEADS, D_MODEL), np.float32)
    for h in range(FOX_HEADS):
        sel[h, h * FOX_HEAD_DIM:(h + 1) * FOX_HEAD_DIM] = 1.0
    return sel


def _fox_decode_kernel(pt_ref, q_ref, kn_ref, vn_ref, lfn_ref, *refs, group):
    ck_refs, cv_refs, clf_refs = refs[0:group], refs[group:2 * group], refs[2 * group:3 * group]
    sel_ref, later_ref, o_ref, qrows_ref, m_ref, l_ref, acc_ref, suf_ref = refs[3 * group:]
    j = pl.program_id(1)

    @pl.when(j == 0)
    def _():
        qrows = (sel_ref[...] * q_ref[0]).astype(BF16)
        qrows_ref[...] = qrows
        kn = jnp.broadcast_to(kn_ref[0], (SUBLANES, D_MODEL)).astype(BF16)
        m_ref[...] = _dot_nt(qrows, kn)[:, 0:1]
        l_ref[...] = jnp.ones_like(l_ref)
        acc_ref[...] = jnp.broadcast_to(vn_ref[0].astype(BF16).astype(F32), acc_ref.shape)
        suf_ref[...] = lfn_ref[0]

    suf = suf_ref[...]
    logits = []
    for g in range(group):
        lf = clf_refs[g][0]
        bias = _dot_tri(lf, later_ref[...]) + suf
        logits.append(_dot(qrows_ref[...], ck_refs[g][0].astype(BF16)) + bias)
        suf = suf + jnp.sum(lf, axis=1, keepdims=True)
    suf_ref[...] = suf
    s = jnp.concatenate(logits, axis=1)
    m_old = m_ref[...]
    m_new = jnp.maximum(m_old, jnp.max(s, axis=1, keepdims=True))
    corr = jnp.exp(m_old - m_new)
    p = jnp.exp(s - m_new)
    l_ref[...] = corr * l_ref[...] + jnp.sum(p, axis=1, keepdims=True)
    m_ref[...] = m_new
    pb = p.astype(BF16)
    pv = _dot_nt(pb[:, 0:PAGE_SIZE], cv_refs[0][0].astype(BF16))
    for g in range(1, group):
        pv = pv + _dot_nt(pb[:, g * PAGE_SIZE:(g + 1) * PAGE_SIZE], cv_refs[g][0].astype(BF16))
    acc_ref[...] = corr * acc_ref[...] + pv

    @pl.when(j == pl.num_programs(1) - 1)
    def _():
        out = jnp.sum(sel_ref[...] * acc_ref[...] * (1.0 / l_ref[...]), axis=0, keepdims=True)
        o_ref[0] = out.astype(BF16)


def _fox_decode(q, k_new, v_new, lf_new, cache_k, cache_v, cache_lf, page_table):
    nb, n_pages = page_table.shape
    n_pool = cache_k.shape[0]
    later = jnp.asarray(np.tril(np.ones((PAGE_SIZE, PAGE_SIZE), np.float32), -1), BF16)
    kt = jnp.transpose(cache_k, (0, 2, 3, 1)).reshape(n_pool, D_MODEL, PAGE_SIZE)
    vt = jnp.transpose(cache_v, (0, 2, 3, 1)).reshape(n_pool, D_MODEL, PAGE_SIZE)
    lft = jnp.transpose(cache_lf, (0, 2, 1))
    group = DECODE_GROUP if n_pages % DECODE_GROUP == 0 else 1

    def page(g):
        return lambda b, j, pt: (pt[b, n_pages - 1 - (j * group + g)], 0, 0)

    def pages(rows):
        return [pl.BlockSpec((1, rows, PAGE_SIZE), page(g)) for g in range(group)]

    whole = lambda shape: pl.BlockSpec(shape, lambda b, j, pt: (0,) * len(shape))
    per_seq = lambda *shape: pl.BlockSpec((1,) + shape, lambda b, j, pt: (b, 0, 0))
    grid_spec = pltpu.PrefetchScalarGridSpec(
        num_scalar_prefetch=1,
        grid=(nb, n_pages // group),
        in_specs=[per_seq(1, D_MODEL), per_seq(1, D_MODEL), per_seq(1, D_MODEL), per_seq(FOX_HEADS, 1),
                  *pages(D_MODEL), *pages(D_MODEL), *pages(FOX_HEADS),
                  whole((FOX_HEADS, D_MODEL)), whole((PAGE_SIZE, PAGE_SIZE))],
        out_specs=per_seq(1, D_MODEL),
        scratch_shapes=[pltpu.VMEM((FOX_HEADS, D_MODEL), BF16),
                        pltpu.VMEM((FOX_HEADS, 1), F32), pltpu.VMEM((FOX_HEADS, 1), F32),
                        pltpu.VMEM((FOX_HEADS, D_MODEL), F32), pltpu.VMEM((FOX_HEADS, 1), F32)],
    )
    row = lambda a: a.astype(F32).reshape(nb, 1, -1)
    out = pl.pallas_call(
        functools.partial(_fox_decode_kernel, group=group),
        grid_spec=grid_spec,
        out_shape=jax.ShapeDtypeStruct((nb, 1, D_MODEL), BF16),
        compiler_params=_params("parallel", "arbitrary"),
        name="fox_decode",
    )(page_table, row(q), row(k_new), row(v_new), lf_new.reshape(nb, FOX_HEADS, 1),
      *([kt] * group), *([vt] * group), *([lft] * group), jnp.asarray(_head_rows()), later)
    return out.reshape(nb, D_MODEL)


def _gla_proj_kernel(x_ref, w_ref, wa1_ref, wa2_ref, ba_ref, q_ref, k_ref, v_ref, r_ref, la_ref):
    xb = x_ref[...].astype(BF16)
    q_ref[...] = _dot(xb, w_ref[:, 0:GLA_DK])
    k_ref[...] = _dot(xb, w_ref[:, GLA_DK:2 * GLA_DK])
    v_ref[...] = _dot(xb, w_ref[:, 2 * GLA_DK:2 * GLA_DK + GLA_DV]).astype(BF16)
    r_ref[...] = _dot(xb, w_ref[:, 2 * GLA_DK + GLA_DV:2 * GLA_DK + 2 * GLA_DV])
    a_lr = _dot(xb, wa1_ref[...]).astype(BF16)
    la_ref[...] = _log_sigmoid(_dot(a_lr, wa2_ref[...]) + ba_ref[...]) / GLA_TAU


def _gla_proj(x, w, w_a1, w_a2, b_a, tm):
    n = x.shape[0]
    row = pl.BlockSpec((tm, D_MODEL), lambda i: (i, 0))
    half = pl.BlockSpec((tm, GLA_DK), lambda i: (i, 0))
    return pl.pallas_call(
        _gla_proj_kernel,
        grid=(n // tm,),
        in_specs=[row, _resident(w.shape), _resident(w_a1.shape), _resident(w_a2.shape), _resident((1, GLA_DK))],
        out_specs=[half, half, row, row, half],
        out_shape=[jax.ShapeDtypeStruct((n, GLA_DK), F32), jax.ShapeDtypeStruct((n, GLA_DK), F32),
                   jax.ShapeDtypeStruct((n, GLA_DV), BF16), jax.ShapeDtypeStruct((n, GLA_DV), F32),
                   jax.ShapeDtypeStruct((n, GLA_DK), F32)],
        compiler_params=_params("parallel"),
        name="gla_proj",
    )(x, w, w_a1, w_a2, b_a)


def _row_to_col(row):
    n = row.shape[1]
    eye = lax.broadcasted_iota(jnp.int32, (n, n), 0) == lax.broadcasted_iota(jnp.int32, (n, n), 1)
    return jnp.sum(jnp.where(eye, row, 0.0), axis=1, keepdims=True)


def _gla_chunk_kernel(q_ref, k_ref, v_ref, la_ref, tri_ref, o_ref, s_out_ref, s_ref, *, n_chunks):
    t = pl.program_id(2)

    @pl.when(t == 0)
    def _():
        s_ref[...] = jnp.zeros_like(s_ref)

    c = GLA_CHUNK
    ts = n_chunks * c
    chunk_rows = [slice(n * c, (n + 1) * c) for n in range(n_chunks)]
    bc = jnp.concatenate([_tri_dot(tri_ref[...], la_ref[r, :]) for r in chunk_rows], axis=0)
    b_last = [bc[r.stop - 1:r.stop, :] for r in chunk_rows]
    b_last_rows = jnp.concatenate([jnp.broadcast_to(b, (c, b.shape[1])) for b in b_last], axis=0)
    k = k_ref[...]
    v = v_ref[...]
    q_dec = (q_ref[...] * GLA_DK_H ** -0.5 * jnp.exp(bc)).astype(BF16)
    k_inv = (k * jnp.exp(-bc)).astype(BF16)
    k_end = (k * jnp.exp(b_last_rows - bc)).astype(BF16)
    row = lax.broadcasted_iota(jnp.int32, (ts, ts), 0)
    col = lax.broadcasted_iota(jnp.int32, (ts, ts), 1)
    same_chunk_causal = (col <= row) & (col >= (row & -c))
    attn = jnp.where(same_chunk_causal, _dot_nt(q_dec, k_inv), 0.0).astype(BF16)
    o_intra = _dot(attn, v)
    kv = [_dot_tn(k_end[r, :], v[r, :]) for r in chunk_rows]
    decay = [_row_to_col(jnp.exp(b)) for b in b_last]
    s = s_ref[...]
    s_before = []
    for n in range(n_chunks):
        s_before.append(s.astype(BF16))
        s = decay[n] * s + kv[n]
    s_ref[...] = s
    for n, r in enumerate(chunk_rows):
        o_ref[r, :] = o_intra[r, :] + _dot(q_dec[r, :], s_before[n])

    @pl.when(t == pl.num_programs(2) - 1)
    def _():
        s_out_ref[0, 0] = s


def _gla_chunked(q, k, v, la, batch, seq, ts):
    nt = seq // ts
    n_chunks = ts // GLA_CHUNK
    tri = jnp.asarray(np.tril(np.ones((GLA_CHUNK, GLA_CHUNK), np.float32)), BF16)
    key = pl.BlockSpec((ts, GLA_DK_H), lambda b, h, t: (b * nt + t, h))
    val = pl.BlockSpec((ts, GLA_DV_H), lambda b, h, t: (b * nt + t, h))
    return pl.pallas_call(
        functools.partial(_gla_chunk_kernel, n_chunks=n_chunks),
        grid=(batch, GLA_HEADS, nt),
        in_specs=[key, key, val, key, _resident((GLA_CHUNK, GLA_CHUNK))],
        out_specs=[val, pl.BlockSpec((1, 1, GLA_DK_H, GLA_DV_H), lambda b, h, t: (b, h, 0, 0))],
        out_shape=[jax.ShapeDtypeStruct((batch * seq, GLA_DV), F32),
                   jax.ShapeDtypeStruct((batch, GLA_HEADS, GLA_DK_H, GLA_DV_H), F32)],
        scratch_shapes=[pltpu.VMEM((GLA_DK_H, GLA_DV_H), F32)],
        compiler_params=_params("parallel", "parallel", "arbitrary"),
        name="gla_chunked",
    )(q, k, v, la, tri)


def _gla_step_kernel(q_ref, k_ref, v_ref, la_ref, s0_ref, o_ref, s_ref):
    for h in range(GLA_HEADS):
        la = la_ref[0, h]
        k = k_ref[0, h]
        v = v_ref[0, h]
        s0 = s0_ref[0, h]
        decay = jnp.exp(la)
        q_dec = (q_ref[0, h] * GLA_DK_H ** -0.5 * decay).astype(BF16)
        k_inv = (k * jnp.exp(-la)).astype(BF16)
        attn = jnp.sum(q_dec.astype(F32) * k_inv.astype(F32), axis=1, keepdims=True).astype(BF16).astype(F32)
        q_rows = jnp.broadcast_to(q_dec.astype(F32), (SUBLANES, GLA_DK_H)).astype(BF16)
        o_ref[0, h] = attn * v + _dot(q_rows, s0.astype(BF16))[0:1, :]
        s_ref[0, h] = _row_to_col(decay) * s0 + _row_to_col(k.astype(BF16).astype(F32)) * v


def _gla_step(q, k, v, la, s0):
    nb = q.shape[0]
    heads = lambda a, width: a.astype(F32).reshape(nb, GLA_HEADS, 1, width)
    key = pl.BlockSpec((1, GLA_HEADS, 1, GLA_DK_H), lambda b: (b, 0, 0, 0))
    val = pl.BlockSpec((1, GLA_HEADS, 1, GLA_DV_H), lambda b: (b, 0, 0, 0))
    state = pl.BlockSpec((1, GLA_HEADS, GLA_DK_H, GLA_DV_H), lambda b: (b, 0, 0, 0))
    o, s = pl.pallas_call(
        _gla_step_kernel,
        grid=(nb,),
        in_specs=[key, key, val, key, state],
        out_specs=[val, state],
        out_shape=[jax.ShapeDtypeStruct((nb, GLA_HEADS, 1, GLA_DV_H), F32),
                   jax.ShapeDtypeStruct((nb, GLA_HEADS, GLA_DK_H, GLA_DV_H), F32)],
        compiler_params=_params("parallel"),
        name="gla_step",
    )(heads(q, GLA_DK_H), heads(k, GLA_DK_H), heads(v, GLA_DV_H), heads(la, GLA_DK_H), s0)
    return o.reshape(nb, GLA_DV), s


PROMPT_TILE = 512
ATTN_Q_TILE = 512
ATTN_K_TILE = 256
AUG_TILE = 512
GLA_TILE = 512


def kernel(x_prompt, x_sample, cache_fox_k, cache_fox_v, cache_fox_logf, state_gla, page_table,
           ln_g, ln_b, ffn_w_in, ffn_w_out, fox_w_in, fox_b_f, fox_w_o,
           gla_w_in, gla_w_a2, gla_b_a, gla_norm_g, gla_w_o):
    batch, seq, _ = x_prompt.shape
    nb = x_sample.shape[0]
    xp = x_prompt.reshape(batch * seq, D_MODEL)
    xs = x_sample.reshape(nb, D_MODEL)
    tp = min(PROMPT_TILE, batch * seq)
    vec = lambda a: a.reshape(1, -1)
    w_in = ffn_w_in.astype(BF16)
    w_out = ffn_w_out.astype(BF16)

    def ffn(i, half, ln):
        return ((w_in, (i, half)), (w_out, (i, half)), vec(ln_g[i, ln]), vec(ln_b[i, ln]))

    xp = _ffn_half(xp, *ffn(0, 0, 0), tp)
    xs = _ffn_half(xs, *ffn(0, 0, 0), nb)
    w_fox = fox_w_in[0].astype(BF16)
    w_qkv, w_f = w_fox[:, :3 * D_MODEL], w_fox[:, 3 * D_MODEL:]
    qt, kt, kbp, vt, vtb, lfp, lft = _fox_proj_t(xp, w_fox.T, w_fox[:, D_MODEL:2 * D_MODEL], w_f,
                                                 vec(fox_b_f[0]), batch, seq, min(PROMPT_TILE, seq))
    qs, ks, vs, _, _, lfs = _fox_proj(xs, w_qkv, w_f, vec(fox_b_f[0]), nb)
    qa, ka = _fox_aug(qt, kbp, lfp, lft, batch, seq, min(AUG_TILE, seq))
    mp = _fox_attn(qa, ka, vtb, batch, seq, min(ATTN_Q_TILE, seq), min(ATTN_K_TILE, seq))
    ms = _fox_decode(qs, ks, vs, lfs, cache_fox_k[0], cache_fox_v[0], cache_fox_logf[0], page_table)
    mix = (fox_w_o[0].astype(BF16), vec(ln_g[0, 1]), vec(ln_b[0, 1])) + ffn(0, 1, 2)
    xp = _mix_ffn(xp, mp, *mix, tp)
    xs = _mix_ffn(xs, ms, *mix, nb)

    xp = _ffn_half(xp, *ffn(1, 0, 0), tp)
    xs = _ffn_half(xs, *ffn(1, 0, 0), nb)
    w_gla = gla_w_in[0].astype(BF16)
    n_main = 2 * GLA_DK + 2 * GLA_DV
    proj = (w_gla[:, :n_main], w_gla[:, n_main:], gla_w_a2[0].astype(BF16), vec(gla_b_a[0]))
    gqp, gkp, gvp, grp, glap = _gla_proj(xp, *proj, tp)
    gqs, gks, gvs, grs, glas = _gla_proj(xs, *proj, nb)
    op, state_p = _gla_chunked(gqp, gkp, gvp, glap, batch, seq, min(GLA_TILE, seq))
    os_, state_s = _gla_step(gqs, gks, gvs, glas, state_gla[0])
    mix = (vec(gla_norm_g[0]), gla_w_o[0].astype(BF16), vec(ln_g[1, 1]), vec(ln_b[1, 1])) + ffn(1, 1, 2)
    xp = _gla_mix_ffn(xp, op, grp, *mix, tp)
    xs = _gla_mix_ffn(xs, os_, grs, *mix, nb)

    heads = (FOX_HEADS, FOX_HEAD_DIM)
    token_major = lambda a: jnp.transpose(a.reshape(batch, *heads, seq), (0, 3, 1, 2))[None]
    return (xp.reshape(batch, seq, D_MODEL), xs.reshape(nb, 1, D_MODEL),
            token_major(kt), token_major(vt), jnp.transpose(lft, (0, 2, 1))[None],
            ks.reshape(1, nb, 1, *heads), vs.reshape(1, nb, 1, *heads), lfs.reshape(1, nb, 1, FOX_HEADS),
            state_p[None], state_s[None])
```

```python
import functools

import numpy as np
import jax
import jax.numpy as jnp
from jax import lax
from jax.experimental import pallas as pl
from jax.experimental.pallas import tpu as pltpu

F32 = jnp.float32
BF16 = jnp.bfloat16

D_MODEL = 1024
DEPTH = 2
PAGE_SIZE = 128
FOX_HEADS = 16
FOX_HEAD_DIM = D_MODEL // FOX_HEADS
FOX_PAIRS = FOX_HEADS // 2
GLA_HEADS = 4
GLA_DK = D_MODEL // 2
GLA_DV = D_MODEL
GLA_DK_H = GLA_DK // GLA_HEADS
GLA_DV_H = GLA_DV // GLA_HEADS
GLA_RANK = 16
GLA_TAU = 16.0
GLA_CHUNK = 64
D_FF = ((8 * D_MODEL // 3 + 127) // 128) * 128
DN_ALPHA = (2 * DEPTH) ** 0.25
LN_EPS = 1e-5
MASK_VALUE = -1e30
LOG2E = 1.4426950408889634

LANES = 128
SUBLANES = 8
MXU_WIDTH = 256
FF_CHUNK = MXU_WIDTH
AUG_PARTS = 3
VMEM_LIMIT = 56 * 2 ** 20
ATTN_ONES_ROWS = 16
ATTN_UNROLL = 4


def _params(*sem):
    return pltpu.CompilerParams(dimension_semantics=sem, vmem_limit_bytes=VMEM_LIMIT)


def _resident(shape):
    return pl.BlockSpec(shape, lambda *_: (0,) * len(shape), pipeline_mode=pl.Buffered(1))


def _stacked_spec(stacked):
    w, index = stacked
    tail = w.shape[len(index):]
    return pl.BlockSpec((None,) * len(index) + tail, lambda *_: index + (0,) * len(tail),
                        pipeline_mode=pl.Buffered(1))


def _dot(a, b):
    return jnp.dot(a, b, preferred_element_type=F32)


def _dot_nt(a, b):
    return lax.dot_general(a, b, (((1,), (1,)), ((), ())), preferred_element_type=F32)


def _dot_tn(a, b):
    return lax.dot_general(a, b, (((0,), (0,)), ((), ())), preferred_element_type=F32)


def _layer_norm(y, g, b):
    mu = jnp.mean(y, axis=-1, keepdims=True)
    d = y - mu
    var = jnp.mean(d * d, axis=-1, keepdims=True)
    return d * lax.rsqrt(var + LN_EPS) * g + b


def _log_sigmoid(z):
    return jnp.minimum(z, 0.0) - jnp.log1p(jnp.exp(-jnp.abs(z)))


def _split3(x):
    hi = x.astype(BF16).astype(F32)
    r = x - hi
    mid = r.astype(BF16).astype(F32)
    lo = (r - mid).astype(BF16).astype(F32)
    return hi, mid, lo


def _tri_dot(tri, x):
    hi, mid, lo = _split3(x)
    return (_dot(tri, hi.astype(BF16)) + _dot(tri, mid.astype(BF16))) + _dot(tri, lo.astype(BF16))


def _dot_tri(x, tri):
    hi, mid, lo = _split3(x)
    return (_dot(hi.astype(BF16), tri) + _dot(mid.astype(BF16), tri)) + _dot(lo.astype(BF16), tri)


def _swiglu_ln(x, win_ref, wout_ref, g, b):
    xb = x.astype(BF16)
    acc = jnp.zeros(x.shape, F32)
    for c in range(D_FF // FF_CHUNK):
        lo = c * FF_CHUNK
        gate = _dot(xb, win_ref[:, lo:lo + FF_CHUNK])
        up = _dot(xb, win_ref[:, D_FF + lo:D_FF + lo + FF_CHUNK])
        h = (gate * jax.nn.sigmoid(gate) * up).astype(BF16)
        acc = acc + _dot(h, wout_ref[lo:lo + FF_CHUNK, :])
    return _layer_norm(DN_ALPHA * x + 0.5 * acc, g, b)


def _ffn_kernel(x_ref, win_ref, wout_ref, g_ref, b_ref, o_ref):
    o_ref[...] = _swiglu_ln(x_ref[...], win_ref, wout_ref, g_ref[...], b_ref[...])


def _ffn_half(x, w_in, w_out, g, b, tm):
    n = x.shape[0]
    row = pl.BlockSpec((tm, D_MODEL), lambda i: (i, 0))
    return pl.pallas_call(
        _ffn_kernel,
        grid=(n // tm,),
        in_specs=[row, _stacked_spec(w_in), _stacked_spec(w_out),
                  _resident((1, D_MODEL)), _resident((1, D_MODEL))],
        out_specs=row,
        out_shape=jax.ShapeDtypeStruct((n, D_MODEL), F32),
        compiler_params=_params("parallel"),
        name="ffn_half",
    )(x, w_in[0], w_out[0], g, b)


def _mix_ffn_kernel(x_ref, m_ref, wo_ref, g1_ref, b1_ref, win_ref, wout_ref, g2_ref, b2_ref, o_ref):
    x1 = _layer_norm(DN_ALPHA * x_ref[...] + _dot(m_ref[...], wo_ref[...]), g1_ref[...], b1_ref[...])
    o_ref[...] = _swiglu_ln(x1, win_ref, wout_ref, g2_ref[...], b2_ref[...])


def _gla_mix_ffn_kernel(x_ref, att_ref, r_ref, gn_ref, wo_ref, g1_ref, b1_ref,
                        win_ref, wout_ref, g2_ref, b2_ref, o_ref):
    heads = []
    for h in range(GLA_HEADS):
        o = att_ref[:, h * GLA_DV_H:(h + 1) * GLA_DV_H]
        mu = jnp.mean(o, axis=-1, keepdims=True)
        d = o - mu
        var = jnp.mean(d * d, axis=-1, keepdims=True)
        heads.append(d * lax.rsqrt(var + LN_EPS))
    on = jnp.concatenate(heads, axis=-1) * gn_ref[...]
    r = r_ref[...]
    m = (on * (r * jax.nn.sigmoid(r))).astype(BF16)
    x1 = _layer_norm(DN_ALPHA * x_ref[...] + _dot(m, wo_ref[...]), g1_ref[...], b1_ref[...])
    o_ref[...] = _swiglu_ln(x1, win_ref, wout_ref, g2_ref[...], b2_ref[...])


def _mix_ffn(x, m, w_o, g1, b1, w_in, w_out, g2, b2, tm):
    n = x.shape[0]
    row = pl.BlockSpec((tm, D_MODEL), lambda i: (i, 0))
    vec = _resident((1, D_MODEL))
    return pl.pallas_call(
        _mix_ffn_kernel,
        grid=(n // tm,),
        in_specs=[row, row, _resident(w_o.shape), vec, vec,
                  _stacked_spec(w_in), _stacked_spec(w_out), vec, vec],
        out_specs=row,
        out_shape=jax.ShapeDtypeStruct((n, D_MODEL), F32),
        compiler_params=_params("parallel"),
        name="mix_ffn",
    )(x, m, w_o, g1, b1, w_in[0], w_out[0], g2, b2)


def _gla_mix_ffn(x, o, r, gn, w_o, g1, b1, w_in, w_out, g2, b2, tm):
    n = x.shape[0]
    row = pl.BlockSpec((tm, D_MODEL), lambda i: (i, 0))
    vec = _resident((1, D_MODEL))
    return pl.pallas_call(
        _gla_mix_ffn_kernel,
        grid=(n // tm,),
        in_specs=[row, row, row, vec, _resident(w_o.shape), vec, vec,
                  _stacked_spec(w_in), _stacked_spec(w_out), vec, vec],
        out_specs=row,
        out_shape=jax.ShapeDtypeStruct((n, D_MODEL), F32),
        compiler_params=_params("parallel"),
        name="gla_mix_ffn",
    )(x, o, r, gn, w_o, g1, b1, w_in[0], w_out[0], g2, b2)


def _fox_proj_kernel(x_ref, wqkv_ref, wf_ref, bf_ref, q_ref, k_ref, v_ref, kb_ref, vb_ref, lf_ref):
    xb = x_ref[...].astype(BF16)
    q = _dot(xb, wqkv_ref[:, 0:D_MODEL])
    q_ref[...] = (q * FOX_HEAD_DIM ** -0.5).astype(BF16)
    k = _dot(xb, wqkv_ref[:, D_MODEL:2 * D_MODEL])
    k_ref[...] = k
    kb_ref[...] = k.astype(BF16)
    v = _dot(xb, wqkv_ref[:, 2 * D_MODEL:3 * D_MODEL])
    v_ref[...] = v
    vb_ref[...] = v.astype(BF16)
    lf_ref[...] = _log_sigmoid(_dot(xb, wf_ref[...]) + bf_ref[...])


def _fox_proj(x, w_qkv, w_f, b_f, tm):
    n = x.shape[0]
    row = pl.BlockSpec((tm, D_MODEL), lambda i: (i, 0))
    gate = pl.BlockSpec((tm, FOX_HEADS), lambda i: (i, 0))
    full = lambda dt: jax.ShapeDtypeStruct((n, D_MODEL), dt)
    return pl.pallas_call(
        _fox_proj_kernel,
        grid=(n // tm,),
        in_specs=[row, _resident(w_qkv.shape), _resident(w_f.shape), _resident((1, FOX_HEADS))],
        out_specs=[row, row, row, row, row, gate],
        out_shape=[full(BF16), full(F32), full(F32), full(BF16), full(BF16),
                   jax.ShapeDtypeStruct((n, FOX_HEADS), F32)],
        compiler_params=_params("parallel"),
        name="fox_proj",
    )(x, w_qkv, w_f, b_f)


def _fox_proj_t_kernel(x_ref, wt_ref, wk_ref, wf_ref, bf_ref, bft_ref,
                       qt_ref, kt_ref, kb_ref, vt_ref, vtb_ref, lf_ref, lft_ref):
    xb = x_ref[...].astype(BF16)
    q_t = _dot_nt(wt_ref[0:D_MODEL, :], xb)
    qt_ref[0] = (q_t * (FOX_HEAD_DIM ** -0.5 * LOG2E)).astype(BF16)
    kt_ref[0] = _dot_nt(wt_ref[D_MODEL:2 * D_MODEL, :], xb)
    kb_ref[...] = _dot(xb, wk_ref[...]).astype(BF16)
    v_t = _dot_nt(wt_ref[2 * D_MODEL:3 * D_MODEL, :], xb)
    vt_ref[0] = v_t
    vtb_ref[0] = v_t.astype(BF16)
    lf_ref[...] = _log_sigmoid(_dot(xb, wf_ref[...]) + bf_ref[...])
    lft_ref[0] = _log_sigmoid(_dot_nt(wt_ref[3 * D_MODEL:3 * D_MODEL + FOX_HEADS, :], xb) + bft_ref[...])


def _fox_proj_t(x, w_t, w_k, w_f, b_f, batch, seq, tm):
    nt = seq // tm
    row = pl.BlockSpec((tm, D_MODEL), lambda b, t: (b * nt + t, 0))
    gate = pl.BlockSpec((tm, FOX_HEADS), lambda b, t: (b * nt + t, 0))
    col = pl.BlockSpec((1, D_MODEL, tm), lambda b, t: (b, 0, t))
    gate_t = pl.BlockSpec((1, FOX_HEADS, tm), lambda b, t: (b, 0, t))
    feat = lambda dt: jax.ShapeDtypeStruct((batch, D_MODEL, seq), dt)
    return pl.pallas_call(
        _fox_proj_t_kernel,
        grid=(batch, nt),
        in_specs=[row, _resident(w_t.shape), _resident(w_k.shape), _resident(w_f.shape),
                  _resident((1, FOX_HEADS)), _resident((FOX_HEADS, 1))],
        out_specs=[col, col, row, col, col, gate, gate_t],
        out_shape=[feat(BF16), feat(F32), jax.ShapeDtypeStruct((batch * seq, D_MODEL), BF16), feat(F32), feat(BF16),
                   jax.ShapeDtypeStruct((batch * seq, FOX_HEADS), F32),
                   jax.ShapeDtypeStruct((batch, FOX_HEADS, seq), F32)],
        compiler_params=_params("parallel", "parallel"),
        name="fox_proj_t",
    )(x, w_t, w_k, w_f, b_f, b_f.reshape(FOX_HEADS, 1))


def _aug_placement():
    pk = np.zeros((AUG_PARTS, FOX_HEADS, D_MODEL), np.float32)
    ones_k = np.zeros((1, D_MODEL), np.float32)
    for h in range(FOX_HEADS):
        base = (h // 2) * LANES + (FOX_HEAD_DIM if h % 2 == 0 else 0)
        for j in range(AUG_PARTS):
            pk[j, h, base + j] = -1.0
            ones_k[0, base + AUG_PARTS + j] = 1.0
    return pk, ones_k


def _fox_aug_kernel(qt_ref, k_ref, lf_ref, lft_ref, tril_ref, triu_ref, pk_ref, ok_ref,
                    qa_ref, ka_ref, carry_row_ref, carry_col_ref):
    @pl.when(pl.program_id(1) == 0)
    def _():
        carry_row_ref[...] = jnp.zeros_like(carry_row_ref)
        carry_col_ref[...] = jnp.zeros_like(carry_col_ref)

    tm = k_ref.shape[0]
    c = _tri_dot(tril_ref[...], lf_ref[...]) + carry_row_ref[...]
    c_t = _dot_tri(lft_ref[0], triu_ref[...]) + carry_col_ref[...]
    carry_row_ref[...] = c[tm - 1:tm, :]
    carry_col_ref[...] = c_t[:, tm - 1:tm]

    aug_k = ok_ref[...]
    for j, part in enumerate(_split3(c * LOG2E)):
        aug_k = aug_k + _dot(part.astype(BF16), pk_ref[j])
    lane = lax.broadcasted_iota(jnp.int32, (tm, D_MODEL), 1) & (LANES - 1)
    low = lane < FOX_HEAD_DIM
    k = k_ref[...].astype(F32)
    k_even = jnp.where(low, k, aug_k).astype(BF16)
    k_odd = jnp.where(low, aug_k, k).astype(BF16)
    for p in range(FOX_PAIRS):
        src = slice(p * LANES, (p + 1) * LANES)
        ka_ref[:, 2 * p * LANES:(2 * p + 1) * LANES] = k_even[:, src]
        ka_ref[:, (2 * p + 1) * LANES:(2 * p + 2) * LANES] = k_odd[:, src]

    parts = _split3(c_t * LOG2E)
    group = 2 * SUBLANES
    r = lax.broadcasted_iota(jnp.int32, (group, tm), 0)
    padding = jnp.zeros((FOX_HEAD_DIM - group, tm), BF16)
    for h in range(FOX_HEADS):
        spare = jnp.where(r < AUG_PARTS, 1.0, 0.0)
        for j, part in enumerate(parts):
            spare = jnp.where(r == AUG_PARTS + j, part[h:h + 1, :], spare)
        feat = slice(h * FOX_HEAD_DIM, (h + 1) * FOX_HEAD_DIM)
        first = slice(h * LANES, h * LANES + FOX_HEAD_DIM)
        second = slice(h * LANES + FOX_HEAD_DIM, (h + 1) * LANES)
        q_rows, spare_rows = (first, second) if h % 2 == 0 else (second, first)
        qa_ref[0, q_rows, :] = qt_ref[0, feat, :]
        qa_ref[0, spare_rows, :] = jnp.concatenate([spare.astype(BF16), padding], axis=0)


def _fox_aug(qt, kb, lf, lft, batch, seq, tm):
    pk, ones_k = _aug_placement()
    tril = jnp.asarray(np.tril(np.ones((tm, tm), np.float32)), BF16)
    triu = jnp.asarray(np.triu(np.ones((tm, tm), np.float32)), BF16)
    nt = seq // tm
    row = pl.BlockSpec((tm, D_MODEL), lambda b, t: (b * nt + t, 0))
    gate = pl.BlockSpec((tm, FOX_HEADS), lambda b, t: (b * nt + t, 0))
    wide = pl.BlockSpec((tm, 2 * D_MODEL), lambda b, t: (b * nt + t, 0))
    col = pl.BlockSpec((1, D_MODEL, tm), lambda b, t: (b, 0, t))
    gate_t = pl.BlockSpec((1, FOX_HEADS, tm), lambda b, t: (b, 0, t))
    tall = pl.BlockSpec((1, 2 * D_MODEL, tm), lambda b, t: (b, 0, t))
    return pl.pallas_call(
        _fox_aug_kernel,
        grid=(batch, nt),
        in_specs=[col, row, gate, gate_t, _resident((tm, tm)), _resident((tm, tm)),
                  _resident(pk.shape), _resident((1, D_MODEL))],
        out_specs=[tall, wide],
        out_shape=[jax.ShapeDtypeStruct((batch, 2 * D_MODEL, seq), BF16),
                   jax.ShapeDtypeStruct((batch * seq, 2 * D_MODEL), BF16)],
        scratch_shapes=[pltpu.VMEM((1, FOX_HEADS), F32), pltpu.VMEM((FOX_HEADS, 1), F32)],
        compiler_params=_params("parallel", "arbitrary"),
        name="fox_aug",
    )(qt, kb, lf, lft, tril, triu, jnp.asarray(pk, BF16), jnp.asarray(ones_k))


def _attention_tile(i, qa_ref, ka_ref, v_ref, o_ref, s0_scr, s1_scr, p0_scr, p1_scr, acc_scr, tq, tk, alongside):
    q_t = (qa_ref[0, 0:LANES, :], qa_ref[0, LANES:2 * LANES, :])
    hd = FOX_HEAD_DIM
    heads = range(2)
    s_scr = (s0_scr, s1_scr)
    p_scr = (p0_scr, p1_scr)
    ones_rows = jnp.ones((ATTN_ONES_ROWS, tk), BF16)

    def logits(b, slot):
        start = pl.multiple_of(b * tk, tk)
        for e in heads:
            s_scr[slot][e] = _dot(ka_ref[pl.ds(start, tk), e * LANES:(e + 1) * LANES], q_t[e])

    def softmax(e, slot, m_old, mask):
        s = s_scr[slot][e] if mask is None else jnp.where(mask, s_scr[slot][e], MASK_VALUE)
        m_new = jnp.maximum(m_old[e], jnp.max(s, axis=0, keepdims=True))
        p_scr[slot][e] = jnp.exp2(s - m_new).astype(BF16)
        return m_new, jnp.exp2(m_old[e] - m_new)

    def weighted_values(b, slot, corr):
        start = pl.multiple_of(b * tk, tk)
        for e in heads:
            v_ext = jnp.concatenate([v_ref[0, e * hd:(e + 1) * hd, pl.ds(start, tk)], ones_rows], axis=0)
            acc_scr[e] = corr[e] * acc_scr[e] + _dot(v_ext, p_scr[slot][e])

    def step(b, slot, m_old, corr_prev, mask, last):
        weighted_values(jnp.maximum(b - 1, 0), 1 - slot, corr_prev)
        first = softmax(0, slot, m_old, mask)
        if not last:
            logits(b + 1, 1 - slot)
        second = softmax(1, slot, m_old, mask)
        return tuple(zip(first, second))

    per_q = tq // tk

    def full_blocks(first, count):
        def body(t, carry):
            for d in range(count):
                carry = step(first + count * t + d, d % 2, *carry, None, False)
            return carry
        return body

    logits(0, 0)
    p1_scr[...] = jnp.zeros_like(p1_scr)
    acc_scr[...] = jnp.zeros_like(acc_scr)
    carry = ((jnp.full((1, tq), MASK_VALUE, F32),) * 2, (jnp.ones((1, tq), F32),) * 2)
    n_full = i * per_q
    long_trip = ATTN_UNROLL * per_q
    n_long = n_full // long_trip
    carry = lax.fori_loop(0, n_long, full_blocks(0, long_trip), carry)
    carry = lax.fori_loop(0, (n_full - n_long * long_trip) // per_q, full_blocks(n_long * long_trip, per_q), carry)
    key = lax.broadcasted_iota(jnp.int32, (tk, tq), 0)
    query = lax.broadcasted_iota(jnp.int32, (tk, tq), 1)
    for d in range(per_q):
        carry = step(n_full + d, d % 2, *carry, key + d * tk <= query, d == per_q - 1)
    alongside()
    weighted_values(n_full + per_q - 1, (per_q - 1) % 2, carry[1])
    out_t = jnp.concatenate([acc_scr[e, 0:hd, :] * (1.0 / acc_scr[e, hd:hd + 1, :]) for e in heads], axis=0)
    o_ref[...] = out_t.T.astype(BF16)


def _head_rows():
    sel = np.zeros((FOX_HEADS, D_MODEL), np.float32)
    for h in range(FOX_HEADS):
        sel[h, h * FOX_HEAD_DIM:(h + 1) * FOX_HEAD_DIM] = 1.0
    return sel


def _fox_mix_kernel(pt_ref, qa_ref, ka_ref, v_ref, q_ref, kn_ref, vn_ref, lfn_ref, *refs,
                    tq, tk, group, steps_per_seq):
    ck_refs, cv_refs, clf_refs = refs[0:group], refs[group:2 * group], refs[2 * group:3 * group]
    (sel_ref, later_ref, o_ref, od_ref, s0_scr, s1_scr, p0_scr, p1_scr, attn_acc_scr,
     qb_ref, m_ref, l_ref, acc_ref, suf_ref, snew_ref) = refs[3 * group:]
    step = (pl.program_id(0) * pl.num_programs(1) + pl.program_id(1)) * pl.num_programs(2) + pl.program_id(2)
    j = lax.rem(step, steps_per_seq)
    hd = FOX_HEAD_DIM

    @pl.when(j == 0)
    def _():
        q_row = q_ref[0]
        for c in range(D_MODEL // LANES):
            chunk = slice(c * LANES, (c + 1) * LANES)
            qb_ref[chunk, :] = jnp.broadcast_to(_row_to_col(q_row[:, chunk]), (LANES, PAGE_SIZE))
        s_new = jnp.sum(sel_ref[...] * (q_row * kn_ref[0]), axis=1, keepdims=True)
        snew_ref[...] = s_new
        m_ref[...] = s_new
        l_ref[...] = jnp.ones_like(l_ref)
        acc_ref[...] = jnp.zeros_like(acc_ref)
        suf_ref[...] = lfn_ref[0]

    def decode_pages():
        suf = suf_ref[...]
        bias = []
        for g in range(group):
            lf = clf_refs[g][0]
            bias.append(_dot_tri(lf, later_ref[...]) + suf)
            suf = suf + jnp.sum(lf, axis=1, keepdims=True)
        suf_ref[...] = suf
        bias = jnp.concatenate(bias, axis=1)
        for h in range(FOX_HEADS):
            rows = slice(h * hd, (h + 1) * hd)
            qb = qb_ref[rows, :]
            s = jnp.concatenate([jnp.sum(ck_refs[g][0, rows, :] * qb, axis=0, keepdims=True)
                                 for g in range(group)], axis=1) + bias[h:h + 1, :]
            m_old = m_ref[h:h + 1, :]
            m_new = jnp.maximum(m_old, jnp.max(s, axis=1, keepdims=True))
            corr = jnp.exp(m_old - m_new)
            p = jnp.exp(s - m_new)
            l_ref[h:h + 1, :] = corr * l_ref[h:h + 1, :] + jnp.sum(p, axis=1, keepdims=True)
            m_ref[h:h + 1, :] = m_new
            acc = corr * acc_ref[rows, :]
            for g in range(group):
                acc = acc + cv_refs[g][0, rows, :] * p[:, g * PAGE_SIZE:(g + 1) * PAGE_SIZE]
            acc_ref[rows, :] = acc

    _attention_tile(pl.program_id(2), qa_ref, ka_ref, v_ref, o_ref, s0_scr, s1_scr, p0_scr, p1_scr,
                    attn_acc_scr, tq, tk, decode_pages)

    @pl.when(j == steps_per_seq - 1)
    def _():
        sel = sel_ref[...]
        w_new = jnp.sum(sel * jnp.exp(snew_ref[...] - m_ref[...]), axis=0, keepdims=True)
        inv_l = jnp.sum(sel * (1.0 / l_ref[...]), axis=0, keepdims=True)
        eye = (lax.broadcasted_iota(jnp.int32, (LANES, LANES), 0)
               == lax.broadcasted_iota(jnp.int32, (LANES, LANES), 1))
        past = []
        for c in range(D_MODEL // LANES):
            col = jnp.sum(acc_ref[c * LANES:(c + 1) * LANES, :], axis=1, keepdims=True)
            past.append(jnp.sum(jnp.where(eye, col, 0.0), axis=0, keepdims=True))
        out = (jnp.concatenate(past, axis=1) + w_new * vn_ref[0]) * inv_l
        od_ref[0] = out.astype(BF16)


def _fox_mix(qa, ka, vtb, batch, seq, tq, tk, q, k_new, v_new, lf_new, cache_k, cache_v, cache_lf, page_table):
    nq = seq // tq
    assert (tq // tk) % 2 == 0, "key blocks alternate between two scratch slots"
    nb, n_pages = page_table.shape
    n_pool = cache_k.shape[0]
    n_steps = batch * FOX_PAIRS * nq
    group, rest = divmod(nb * n_pages, n_steps)
    assert rest == 0 and group > 0 and n_pages % group == 0, "cache pages must spread evenly over the grid steps"
    steps_per_seq = n_pages // group
    later = jnp.asarray(np.tril(np.ones((PAGE_SIZE, PAGE_SIZE), np.float32), -1), BF16)
    kt = jnp.transpose(cache_k, (0, 2, 3, 1)).reshape(n_pool, D_MODEL, PAGE_SIZE)
    vt = jnp.transpose(cache_v, (0, 2, 3, 1)).reshape(n_pool, D_MODEL, PAGE_SIZE)
    lft = jnp.transpose(cache_lf, (0, 2, 1))

    def step_of(b, p, i):
        return (b * FOX_PAIRS + p) * nq + i

    def page(g):
        def index(b, p, i, pt):
            s = step_of(b, p, i)
            return (pt[s // steps_per_seq, n_pages - 1 - ((s % steps_per_seq) * group + g)], 0, 0)
        return index

    def pages(rows):
        return [pl.BlockSpec((1, rows, PAGE_SIZE), page(g)) for g in range(group)]

    whole = lambda shape: pl.BlockSpec(shape, lambda b, p, i, pt: (0,) * len(shape))
    per_seq = lambda *shape: pl.BlockSpec((1,) + shape, lambda b, p, i, pt: (step_of(b, p, i) // steps_per_seq, 0, 0))
    grid_spec = pltpu.PrefetchScalarGridSpec(
        num_scalar_prefetch=1,
        grid=(batch, FOX_PAIRS, nq),
        in_specs=[pl.BlockSpec((1, 2 * LANES, tq), lambda b, p, i, pt: (b, p, i)),
                  pl.BlockSpec((seq, 2 * LANES), lambda b, p, i, pt: (b, p)),
                  pl.BlockSpec((1, LANES, seq), lambda b, p, i, pt: (b, p, 0)),
                  per_seq(1, D_MODEL), per_seq(1, D_MODEL), per_seq(1, D_MODEL), per_seq(FOX_HEADS, 1),
                  *pages(D_MODEL), *pages(D_MODEL), *pages(FOX_HEADS),
                  whole((FOX_HEADS, D_MODEL)), whole((PAGE_SIZE, PAGE_SIZE))],
        out_specs=[pl.BlockSpec((tq, LANES), lambda b, p, i, pt: (b * nq + i, p)), per_seq(1, D_MODEL)],
        scratch_shapes=[pltpu.VMEM((2, tk, tq), F32), pltpu.VMEM((2, tk, tq), F32),
                        pltpu.VMEM((2, tk, tq), BF16), pltpu.VMEM((2, tk, tq), BF16),
                        pltpu.VMEM((2, FOX_HEAD_DIM + ATTN_ONES_ROWS, tq), F32),
                        pltpu.VMEM((D_MODEL, PAGE_SIZE), F32),
                        pltpu.VMEM((FOX_HEADS, 1), F32), pltpu.VMEM((FOX_HEADS, 1), F32),
                        pltpu.VMEM((D_MODEL, PAGE_SIZE), F32),
                        pltpu.VMEM((FOX_HEADS, 1), F32), pltpu.VMEM((FOX_HEADS, 1), F32)],
    )
    row = lambda a: a.astype(F32).reshape(nb, 1, -1)
    attn, out = pl.pallas_call(
        functools.partial(_fox_mix_kernel, tq=tq, tk=tk, group=group, steps_per_seq=steps_per_seq),
        grid_spec=grid_spec,
        out_shape=[jax.ShapeDtypeStruct((batch * seq, D_MODEL), BF16),
                   jax.ShapeDtypeStruct((nb, 1, D_MODEL), BF16)],
        compiler_params=_params("arbitrary", "arbitrary", "arbitrary"),
        name="fox_mix",
    )(page_table, qa, ka, vtb, row(q), row(k_new), row(v_new), lf_new.reshape(nb, FOX_HEADS, 1),
      *([kt] * group), *([vt] * group), *([lft] * group), jnp.asarray(_head_rows()), later)
    return attn, out.reshape(nb, D_MODEL)


def _gla_proj_kernel(x_ref, w_ref, wa1_ref, wa2_ref, ba_ref, q_ref, k_ref, v_ref, r_ref, la_ref):
    xb = x_ref[...].astype(BF16)
    q_ref[...] = _dot(xb, w_ref[:, 0:GLA_DK])
    k_ref[...] = _dot(xb, w_ref[:, GLA_DK:2 * GLA_DK])
    v_ref[...] = _dot(xb, w_ref[:, 2 * GLA_DK:2 * GLA_DK + GLA_DV]).astype(BF16)
    r_ref[...] = _dot(xb, w_ref[:, 2 * GLA_DK + GLA_DV:2 * GLA_DK + 2 * GLA_DV])
    a_lr = _dot(xb, wa1_ref[...]).astype(BF16)
    la_ref[...] = _log_sigmoid(_dot(a_lr, wa2_ref[...]) + ba_ref[...]) / GLA_TAU


def _gla_proj(x, w, w_a1, w_a2, b_a, tm):
    n = x.shape[0]
    row = pl.BlockSpec((tm, D_MODEL), lambda i: (i, 0))
    half = pl.BlockSpec((tm, GLA_DK), lambda i: (i, 0))
    return pl.pallas_call(
        _gla_proj_kernel,
        grid=(n // tm,),
        in_specs=[row, _resident(w.shape), _resident(w_a1.shape), _resident(w_a2.shape), _resident((1, GLA_DK))],
        out_specs=[half, half, row, row, half],
        out_shape=[jax.ShapeDtypeStruct((n, GLA_DK), F32), jax.ShapeDtypeStruct((n, GLA_DK), F32),
                   jax.ShapeDtypeStruct((n, GLA_DV), BF16), jax.ShapeDtypeStruct((n, GLA_DV), F32),
                   jax.ShapeDtypeStruct((n, GLA_DK), F32)],
        compiler_params=_params("parallel"),
        name="gla_proj",
    )(x, w, w_a1, w_a2, b_a)


def _row_to_col(row):
    n = row.shape[1]
    eye = lax.broadcasted_iota(jnp.int32, (n, n), 0) == lax.broadcasted_iota(jnp.int32, (n, n), 1)
    return jnp.sum(jnp.where(eye, row, 0.0), axis=1, keepdims=True)


def _gla_chunk_kernel(q_ref, k_ref, v_ref, la_ref, tri_ref, o_ref, s_out_ref, s_ref, *, n_chunks):
    t = pl.program_id(2)

    @pl.when(t == 0)
    def _():
        s_ref[...] = jnp.zeros_like(s_ref)

    c = GLA_CHUNK
    ts = n_chunks * c
    chunk_rows = [slice(n * c, (n + 1) * c) for n in range(n_chunks)]
    bc = jnp.concatenate([_tri_dot(tri_ref[...], la_ref[r, :]) for r in chunk_rows], axis=0)
    b_last = [bc[r.stop - 1:r.stop, :] for r in chunk_rows]
    b_last_rows = jnp.concatenate([jnp.broadcast_to(b, (c, b.shape[1])) for b in b_last], axis=0)
    k = k_ref[...]
    v = v_ref[...]
    q_dec = (q_ref[...] * GLA_DK_H ** -0.5 * jnp.exp(bc)).astype(BF16)
    k_inv = (k * jnp.exp(-bc)).astype(BF16)
    k_end = (k * jnp.exp(b_last_rows - bc)).astype(BF16)
    row = lax.broadcasted_iota(jnp.int32, (ts, ts), 0)
    col = lax.broadcasted_iota(jnp.int32, (ts, ts), 1)
    same_chunk_causal = (col <= row) & (col >= (row & -c))
    attn = jnp.where(same_chunk_causal, _dot_nt(q_dec, k_inv), 0.0).astype(BF16)
    o_intra = _dot(attn, v)
    kv = [_dot_tn(k_end[r, :], v[r, :]) for r in chunk_rows]
    decay = [_row_to_col(jnp.exp(b)) for b in b_last]
    s = s_ref[...]
    s_before = []
    for n in range(n_chunks):
        s_before.append(s.astype(BF16))
        s = decay[n] * s + kv[n]
    s_ref[...] = s
    for n, r in enumerate(chunk_rows):
        o_ref[r, :] = o_intra[r, :] + _dot(q_dec[r, :], s_before[n])

    @pl.when(t == pl.num_programs(2) - 1)
    def _():
        s_out_ref[0, 0] = s


def _gla_chunked(q, k, v, la, batch, seq, ts):
    nt = seq // ts
    n_chunks = ts // GLA_CHUNK
    tri = jnp.asarray(np.tril(np.ones((GLA_CHUNK, GLA_CHUNK), np.float32)), BF16)
    key = pl.BlockSpec((ts, GLA_DK_H), lambda b, h, t: (b * nt + t, h))
    val = pl.BlockSpec((ts, GLA_DV_H), lambda b, h, t: (b * nt + t, h))
    return pl.pallas_call(
        functools.partial(_gla_chunk_kernel, n_chunks=n_chunks),
        grid=(batch, GLA_HEADS, nt),
        in_specs=[key, key, val, key, _resident((GLA_CHUNK, GLA_CHUNK))],
        out_specs=[val, pl.BlockSpec((1, 1, GLA_DK_H, GLA_DV_H), lambda b, h, t: (b, h, 0, 0))],
        out_shape=[jax.ShapeDtypeStruct((batch * seq, GLA_DV), F32),
                   jax.ShapeDtypeStruct((batch, GLA_HEADS, GLA_DK_H, GLA_DV_H), F32)],
        scratch_shapes=[pltpu.VMEM((GLA_DK_H, GLA_DV_H), F32)],
        compiler_params=_params("parallel", "parallel", "arbitrary"),
        name="gla_chunked",
    )(q, k, v, la, tri)


def _gla_step_kernel(q_ref, k_ref, v_ref, la_ref, s0_ref, o_ref, s_ref):
    for h in range(GLA_HEADS):
        la = la_ref[0, h]
        k = k_ref[0, h]
        v = v_ref[0, h]
        s0 = s0_ref[0, h]
        decay = jnp.exp(la)
        q_dec = (q_ref[0, h] * GLA_DK_H ** -0.5 * decay).astype(BF16)
        k_inv = (k * jnp.exp(-la)).astype(BF16)
        attn = jnp.sum(q_dec.astype(F32) * k_inv.astype(F32), axis=1, keepdims=True).astype(BF16).astype(F32)
        q_rows = jnp.broadcast_to(q_dec.astype(F32), (SUBLANES, GLA_DK_H)).astype(BF16)
        o_ref[0, h] = attn * v + _dot(q_rows, s0.astype(BF16))[0:1, :]
        s_ref[0, h] = _row_to_col(decay) * s0 + _row_to_col(k.astype(BF16).astype(F32)) * v


def _gla_step(q, k, v, la, s0):
    nb = q.shape[0]
    heads = lambda a, width: a.astype(F32).reshape(nb, GLA_HEADS, 1, width)
    key = pl.BlockSpec((1, GLA_HEADS, 1, GLA_DK_H), lambda b: (b, 0, 0, 0))
    val = pl.BlockSpec((1, GLA_HEADS, 1, GLA_DV_H), lambda b: (b, 0, 0, 0))
    state = pl.BlockSpec((1, GLA_HEADS, GLA_DK_H, GLA_DV_H), lambda b: (b, 0, 0, 0))
    o, s = pl.pallas_call(
        _gla_step_kernel,
        grid=(nb,),
        in_specs=[key, key, val, key, state],
        out_specs=[val, state],
        out_shape=[jax.ShapeDtypeStruct((nb, GLA_HEADS, 1, GLA_DV_H), F32),
                   jax.ShapeDtypeStruct((nb, GLA_HEADS, GLA_DK_H, GLA_DV_H), F32)],
        compiler_params=_params("parallel"),
        name="gla_step",
    )(heads(q, GLA_DK_H), heads(k, GLA_DK_H), heads(v, GLA_DV_H), heads(la, GLA_DK_H), s0)
    return o.reshape(nb, GLA_DV), s


PROMPT_TILE = 512
ATTN_Q_TILE = 512
ATTN_K_TILE = 256
AUG_TILE = 512
GLA_TILE = 512


def kernel(x_prompt, x_sample, cache_fox_k, cache_fox_v, cache_fox_logf, state_gla, page_table,
           ln_g, ln_b, ffn_w_in, ffn_w_out, fox_w_in, fox_b_f, fox_w_o,
           gla_w_in, gla_w_a2, gla_b_a, gla_norm_g, gla_w_o):
    batch, seq, _ = x_prompt.shape
    nb = x_sample.shape[0]
    xp = x_prompt.reshape(batch * seq, D_MODEL)
    xs = x_sample.reshape(nb, D_MODEL)
    tp = min(PROMPT_TILE, batch * seq)
    vec = lambda a: a.reshape(1, -1)
    w_in = ffn_w_in.astype(BF16)
    w_out = ffn_w_out.astype(BF16)

    def ffn(i, half, ln):
        return ((w_in, (i, half)), (w_out, (i, half)), vec(ln_g[i, ln]), vec(ln_b[i, ln]))

    xp = _ffn_half(xp, *ffn(0, 0, 0), tp)
    xs = _ffn_half(xs, *ffn(0, 0, 0), nb)
    w_fox = fox_w_in[0].astype(BF16)
    w_qkv, w_f = w_fox[:, :3 * D_MODEL], w_fox[:, 3 * D_MODEL:]
    qt, kt, kbp, vt, vtb, lfp, lft = _fox_proj_t(xp, w_fox.T, w_fox[:, D_MODEL:2 * D_MODEL], w_f,
                                                 vec(fox_b_f[0]), batch, seq, min(PROMPT_TILE, seq))
    qs, ks, vs, _, _, lfs = _fox_proj(xs, w_qkv, w_f, vec(fox_b_f[0]), nb)
    qa, ka = _fox_aug(qt, kbp, lfp, lft, batch, seq, min(AUG_TILE, seq))
    mp, ms = _fox_mix(qa, ka, vtb, batch, seq, min(ATTN_Q_TILE, seq), min(ATTN_K_TILE, seq),
                      qs, ks, vs, lfs, cache_fox_k[0], cache_fox_v[0], cache_fox_logf[0], page_table)
    mix = (fox_w_o[0].astype(BF16), vec(ln_g[0, 1]), vec(ln_b[0, 1])) + ffn(0, 1, 2)
    xp = _mix_ffn(xp, mp, *mix, tp)
    xs = _mix_ffn(xs, ms, *mix, nb)

    xp = _ffn_half(xp, *ffn(1, 0, 0), tp)
    xs = _ffn_half(xs, *ffn(1, 0, 0), nb)
    w_gla = gla_w_in[0].astype(BF16)
    n_main = 2 * GLA_DK + 2 * GLA_DV
    proj = (w_gla[:, :n_main], w_gla[:, n_main:], gla_w_a2[0].astype(BF16), vec(gla_b_a[0]))
    gqp, gkp, gvp, grp, glap = _gla_proj(xp, *proj, tp)
    gqs, gks, gvs, grs, glas = _gla_proj(xs, *proj, nb)
    op, state_p = _gla_chunked(gqp, gkp, gvp, glap, batch, seq, min(GLA_TILE, seq))
    os_, state_s = _gla_step(gqs, gks, gvs, glas, state_gla[0])
    mix = (vec(gla_norm_g[0]), gla_w_o[0].astype(BF16), vec(ln_g[1, 1]), vec(ln_b[1, 1])) + ffn(1, 1, 2)
    xp = _gla_mix_ffn(xp, op, grp, *mix, tp)
    xs = _gla_mix_ffn(xs, os_, grs, *mix, nb)

    heads = (FOX_HEADS, FOX_HEAD_DIM)
    token_major = lambda a: jnp.transpose(a.reshape(batch, *heads, seq), (0, 3, 1, 2))[None]
    return (xp.reshape(batch, seq, D_MODEL), xs.reshape(nb, 1, D_MODEL),
            token_major(kt), token_major(vt), jnp.transpose(lft, (0, 2, 1))[None],
            ks.reshape(1, nb, 1, *heads), vs.reshape(1, nb, 1, *heads), lfs.reshape(1, nb, 1, FOX_HEADS),
            state_p[None], state_s[None])
```

```python
import functools

import numpy as np
import jax
import jax.numpy as jnp
from jax import lax
from jax.experimental import pallas as pl
from jax.experimental.pallas import tpu as pltpu

F32 = jnp.float32
BF16 = jnp.bfloat16

D_MODEL = 1024
DEPTH = 2
PAGE_SIZE = 128
FOX_HEADS = 16
FOX_HEAD_DIM = D_MODEL // FOX_HEADS
FOX_PAIRS = FOX_HEADS // 2
GLA_HEADS = 4
GLA_DK = D_MODEL // 2
GLA_DV = D_MODEL
GLA_DK_H = GLA_DK // GLA_HEADS
GLA_DV_H = GLA_DV // GLA_HEADS
GLA_RANK = 16
GLA_TAU = 16.0
GLA_CHUNK = 64
D_FF = ((8 * D_MODEL // 3 + 127) // 128) * 128
DN_ALPHA = (2 * DEPTH) ** 0.25
LN_EPS = 1e-5
MASK_VALUE = -1e30
LOG2E = 1.4426950408889634

LANES = 128
SUBLANES = 8
MXU_WIDTH = 256
FF_CHUNK = MXU_WIDTH
AUG_PARTS = 3
VMEM_LIMIT = 56 * 2 ** 20
ATTN_ONES_ROWS = 16
ATTN_UNROLL = 4


def _params(*sem):
    return pltpu.CompilerParams(dimension_semantics=sem, vmem_limit_bytes=VMEM_LIMIT)


def _resident(shape):
    return pl.BlockSpec(shape, lambda *_: (0,) * len(shape), pipeline_mode=pl.Buffered(1))


def _stacked_spec(stacked):
    w, index = stacked
    tail = w.shape[len(index):]
    return pl.BlockSpec((None,) * len(index) + tail, lambda *_: index + (0,) * len(tail),
                        pipeline_mode=pl.Buffered(1))


def _dot(a, b):
    return jnp.dot(a, b, preferred_element_type=F32)


def _dot_nt(a, b):
    return lax.dot_general(a, b, (((1,), (1,)), ((), ())), preferred_element_type=F32)


def _dot_tn(a, b):
    return lax.dot_general(a, b, (((0,), (0,)), ((), ())), preferred_element_type=F32)


def _layer_norm(y, g, b):
    mu = jnp.mean(y, axis=-1, keepdims=True)
    d = y - mu
    var = jnp.mean(d * d, axis=-1, keepdims=True)
    return d * lax.rsqrt(var + LN_EPS) * g + b


def _log_sigmoid(z):
    return jnp.minimum(z, 0.0) - jnp.log1p(jnp.exp(-jnp.abs(z)))


def _split3(x):
    hi = x.astype(BF16).astype(F32)
    r = x - hi
    mid = r.astype(BF16).astype(F32)
    lo = (r - mid).astype(BF16).astype(F32)
    return hi, mid, lo


def _tri_dot(tri, x):
    hi, mid, lo = _split3(x)
    return (_dot(tri, hi.astype(BF16)) + _dot(tri, mid.astype(BF16))) + _dot(tri, lo.astype(BF16))


def _dot_tri(x, tri):
    hi, mid, lo = _split3(x)
    return (_dot(hi.astype(BF16), tri) + _dot(mid.astype(BF16), tri)) + _dot(lo.astype(BF16), tri)


def _swiglu_ln(x, win_ref, wout_ref, g, b):
    xb = x.astype(BF16)
    acc = jnp.zeros(x.shape, F32)
    for c in range(D_FF // FF_CHUNK):
        lo = c * FF_CHUNK
        gate = _dot(xb, win_ref[:, lo:lo + FF_CHUNK])
        up = _dot(xb, win_ref[:, D_FF + lo:D_FF + lo + FF_CHUNK])
        h = (gate * jax.nn.sigmoid(gate) * up).astype(BF16)
        acc = acc + _dot(h, wout_ref[lo:lo + FF_CHUNK, :])
    return _layer_norm(DN_ALPHA * x + 0.5 * acc, g, b)


def _on_last_step(fn):
    pl.when(pl.program_id(0) == pl.num_programs(0) - 1)(fn)


def _sample_rows(a):
    return pl.BlockSpec(a.shape, lambda i: (0,) * len(a.shape))


def _ffn_kernel(x_ref, xs_ref, win_ref, wout_ref, g_ref, b_ref, o_ref, os_ref):
    def rows(x_ref, o_ref):
        o_ref[...] = _swiglu_ln(x_ref[...], win_ref, wout_ref, g_ref[...], b_ref[...])

    rows(x_ref, o_ref)
    _on_last_step(lambda: rows(xs_ref, os_ref))


def _ffn_half(x, xs, w_in, w_out, g, b, tm):
    n = x.shape[0]
    row = pl.BlockSpec((tm, D_MODEL), lambda i: (i, 0))
    return pl.pallas_call(
        _ffn_kernel,
        grid=(n // tm,),
        in_specs=[row, _sample_rows(xs), _stacked_spec(w_in), _stacked_spec(w_out),
                  _resident((1, D_MODEL)), _resident((1, D_MODEL))],
        out_specs=[row, _sample_rows(xs)],
        out_shape=[jax.ShapeDtypeStruct((n, D_MODEL), F32), jax.ShapeDtypeStruct(xs.shape, F32)],
        compiler_params=_params("arbitrary"),
        name="ffn_half",
    )(x, xs, w_in[0], w_out[0], g, b)


def _mix_ffn_kernel(x_ref, m_ref, xs_ref, ms_ref, wo_ref, g1_ref, b1_ref, win_ref, wout_ref, g2_ref, b2_ref,
                    o_ref, os_ref):
    def rows(x_ref, m_ref, o_ref):
        x1 = _layer_norm(DN_ALPHA * x_ref[...] + _dot(m_ref[...], wo_ref[...]), g1_ref[...], b1_ref[...])
        o_ref[...] = _swiglu_ln(x1, win_ref, wout_ref, g2_ref[...], b2_ref[...])

    rows(x_ref, m_ref, o_ref)
    _on_last_step(lambda: rows(xs_ref, ms_ref, os_ref))


def _gla_mix_ffn_kernel(x_ref, att_ref, r_ref, xs_ref, atts_ref, rs_ref, gn_ref, wo_ref, g1_ref, b1_ref,
                        win_ref, wout_ref, g2_ref, b2_ref, o_ref, os_ref):
    refs = (gn_ref, wo_ref, g1_ref, b1_ref, win_ref, wout_ref, g2_ref, b2_ref)
    _gla_mix_rows(x_ref, att_ref, r_ref, *refs, o_ref)
    _on_last_step(lambda: _gla_mix_rows(xs_ref, atts_ref, rs_ref, *refs, os_ref))


def _gla_mix_rows(x_ref, att_ref, r_ref, gn_ref, wo_ref, g1_ref, b1_ref, win_ref, wout_ref, g2_ref, b2_ref, o_ref):
    heads = []
    for h in range(GLA_HEADS):
        o = att_ref[:, h * GLA_DV_H:(h + 1) * GLA_DV_H]
        mu = jnp.mean(o, axis=-1, keepdims=True)
        d = o - mu
        var = jnp.mean(d * d, axis=-1, keepdims=True)
        heads.append(d * lax.rsqrt(var + LN_EPS))
    on = jnp.concatenate(heads, axis=-1) * gn_ref[...]
    r = r_ref[...]
    m = (on * (r * jax.nn.sigmoid(r))).astype(BF16)
    x1 = _layer_norm(DN_ALPHA * x_ref[...] + _dot(m, wo_ref[...]), g1_ref[...], b1_ref[...])
    o_ref[...] = _swiglu_ln(x1, win_ref, wout_ref, g2_ref[...], b2_ref[...])


def _mix_ffn(x, m, xs, ms, w_o, g1, b1, w_in, w_out, g2, b2, tm):
    n = x.shape[0]
    row = pl.BlockSpec((tm, D_MODEL), lambda i: (i, 0))
    vec = _resident((1, D_MODEL))
    return pl.pallas_call(
        _mix_ffn_kernel,
        grid=(n // tm,),
        in_specs=[row, row, _sample_rows(xs), _sample_rows(ms), _resident(w_o.shape), vec, vec,
                  _stacked_spec(w_in), _stacked_spec(w_out), vec, vec],
        out_specs=[row, _sample_rows(xs)],
        out_shape=[jax.ShapeDtypeStruct((n, D_MODEL), F32), jax.ShapeDtypeStruct(xs.shape, F32)],
        compiler_params=_params("arbitrary"),
        name="mix_ffn",
    )(x, m, xs, ms, w_o, g1, b1, w_in[0], w_out[0], g2, b2)


def _gla_mix_ffn(x, o, r, xs, os_, rs, gn, w_o, g1, b1, w_in, w_out, g2, b2, tm):
    n = x.shape[0]
    row = pl.BlockSpec((tm, D_MODEL), lambda i: (i, 0))
    vec = _resident((1, D_MODEL))
    return pl.pallas_call(
        _gla_mix_ffn_kernel,
        grid=(n // tm,),
        in_specs=[row, row, row, _sample_rows(xs), _sample_rows(os_), _sample_rows(rs),
                  vec, _resident(w_o.shape), vec, vec,
                  _stacked_spec(w_in), _stacked_spec(w_out), vec, vec],
        out_specs=[row, _sample_rows(xs)],
        out_shape=[jax.ShapeDtypeStruct((n, D_MODEL), F32), jax.ShapeDtypeStruct(xs.shape, F32)],
        compiler_params=_params("arbitrary"),
        name="gla_mix_ffn",
    )(x, o, r, xs, os_, rs, gn, w_o, g1, b1, w_in[0], w_out[0], g2, b2)


def _fox_proj_kernel(x_ref, wqkv_ref, wf_ref, bf_ref, q_ref, k_ref, v_ref, lf_ref):
    xb = x_ref[...].astype(BF16)
    q = _dot(xb, wqkv_ref[:, 0:D_MODEL])
    q_ref[...] = (q * FOX_HEAD_DIM ** -0.5).astype(BF16)
    k_ref[...] = _dot(xb, wqkv_ref[:, D_MODEL:2 * D_MODEL])
    v_ref[...] = _dot(xb, wqkv_ref[:, 2 * D_MODEL:3 * D_MODEL])
    lf_ref[...] = _log_sigmoid(_dot(xb, wf_ref[...]) + bf_ref[...])


def _fox_proj(x, w_qkv, w_f, b_f, tm):
    n = x.shape[0]
    row = pl.BlockSpec((tm, D_MODEL), lambda i: (i, 0))
    gate = pl.BlockSpec((tm, FOX_HEADS), lambda i: (i, 0))
    full = lambda dt: jax.ShapeDtypeStruct((n, D_MODEL), dt)
    return pl.pallas_call(
        _fox_proj_kernel,
        grid=(n // tm,),
        in_specs=[row, _resident(w_qkv.shape), _resident(w_f.shape), _resident((1, FOX_HEADS))],
        out_specs=[row, row, row, gate],
        out_shape=[full(BF16), full(F32), full(F32), jax.ShapeDtypeStruct((n, FOX_HEADS), F32)],
        compiler_params=_params("parallel"),
        name="fox_proj",
    )(x, w_qkv, w_f, b_f)


def _fox_proj_t_kernel(x_ref, wt_ref, wk_ref, wf_ref, bf_ref, bft_ref, tril_ref, triu_ref, pk_ref, ok_ref,
                       kt_ref, vt_ref, vtb_ref, lft_ref, qa_ref, ka_ref, carry_row_ref, carry_col_ref):
    xb = x_ref[...].astype(BF16)
    q_t = (_dot_nt(wt_ref[0:D_MODEL, :], xb) * (FOX_HEAD_DIM ** -0.5 * LOG2E)).astype(BF16)
    kt_ref[0] = _dot_nt(wt_ref[D_MODEL:2 * D_MODEL, :], xb)
    k = _dot(xb, wk_ref[...])
    v_t = _dot_nt(wt_ref[2 * D_MODEL:3 * D_MODEL, :], xb)
    vt_ref[0] = v_t
    vtb_ref[0] = v_t.astype(BF16)
    lf = _log_sigmoid(_dot(xb, wf_ref[...]) + bf_ref[...])
    lf_t = _log_sigmoid(_dot_nt(wt_ref[3 * D_MODEL:3 * D_MODEL + FOX_HEADS, :], xb) + bft_ref[...])
    lft_ref[0] = lf_t
    _write_augmented(q_t, k, lf, lf_t, tril_ref, triu_ref, pk_ref, ok_ref, qa_ref, ka_ref,
                     carry_row_ref, carry_col_ref)


def _fox_proj_t(x, w_t, w_k, w_f, b_f, batch, seq, tm):
    pk, ones_k = _aug_placement()
    tril = jnp.asarray(np.tril(np.ones((tm, tm), np.float32)), BF16)
    triu = jnp.asarray(np.triu(np.ones((tm, tm), np.float32)), BF16)
    nt = seq // tm
    row = pl.BlockSpec((tm, D_MODEL), lambda b, t: (b * nt + t, 0))
    wide = pl.BlockSpec((tm, 2 * D_MODEL), lambda b, t: (b * nt + t, 0))
    col = pl.BlockSpec((1, D_MODEL, tm), lambda b, t: (b, 0, t))
    tall = pl.BlockSpec((1, 2 * D_MODEL, tm), lambda b, t: (b, 0, t))
    gate_t = pl.BlockSpec((1, FOX_HEADS, tm), lambda b, t: (b, 0, t))
    feat = lambda dt: jax.ShapeDtypeStruct((batch, D_MODEL, seq), dt)
    return pl.pallas_call(
        _fox_proj_t_kernel,
        grid=(batch, nt),
        in_specs=[row, _resident(w_t.shape), _resident(w_k.shape), _resident(w_f.shape),
                  _resident((1, FOX_HEADS)), _resident((FOX_HEADS, 1)),
                  _resident((tm, tm)), _resident((tm, tm)), _resident(pk.shape), _resident((1, D_MODEL))],
        out_specs=[col, col, col, gate_t, tall, wide],
        out_shape=[feat(F32), feat(F32), feat(BF16), jax.ShapeDtypeStruct((batch, FOX_HEADS, seq), F32),
                   jax.ShapeDtypeStruct((batch, 2 * D_MODEL, seq), BF16),
                   jax.ShapeDtypeStruct((batch * seq, 2 * D_MODEL), BF16)],
        scratch_shapes=[pltpu.VMEM((1, FOX_HEADS), F32), pltpu.VMEM((FOX_HEADS, 1), F32)],
        compiler_params=_params("parallel", "arbitrary"),
        name="fox_proj_t",
    )(x, w_t, w_k, w_f, b_f, b_f.reshape(FOX_HEADS, 1), tril, triu, jnp.asarray(pk, BF16), jnp.asarray(ones_k))


def _aug_placement():
    pk = np.zeros((AUG_PARTS, FOX_HEADS, D_MODEL), np.float32)
    ones_k = np.zeros((1, D_MODEL), np.float32)
    for h in range(FOX_HEADS):
        base = (h // 2) * LANES + (FOX_HEAD_DIM if h % 2 == 0 else 0)
        for j in range(AUG_PARTS):
            pk[j, h, base + j] = -1.0
            ones_k[0, base + AUG_PARTS + j] = 1.0
    return pk, ones_k


def _write_augmented(q_t, k, lf, lf_t, tril_ref, triu_ref, pk_ref, ok_ref,
                     qa_ref, ka_ref, carry_row_ref, carry_col_ref):
    @pl.when(pl.program_id(1) == 0)
    def _():
        carry_row_ref[...] = jnp.zeros_like(carry_row_ref)
        carry_col_ref[...] = jnp.zeros_like(carry_col_ref)

    tm = k.shape[0]
    c = _tri_dot(tril_ref[...], lf) + carry_row_ref[...]
    c_t = _dot_tri(lf_t, triu_ref[...]) + carry_col_ref[...]
    carry_row_ref[...] = c[tm - 1:tm, :]
    carry_col_ref[...] = c_t[:, tm - 1:tm]

    aug_k = ok_ref[...]
    for j, part in enumerate(_split3(c * LOG2E)):
        aug_k = aug_k + _dot(part.astype(BF16), pk_ref[j])
    lane = lax.broadcasted_iota(jnp.int32, (tm, D_MODEL), 1) & (LANES - 1)
    low = lane < FOX_HEAD_DIM
    k_even = jnp.where(low, k, aug_k).astype(BF16)
    k_odd = jnp.where(low, aug_k, k).astype(BF16)
    for p in range(FOX_PAIRS):
        src = slice(p * LANES, (p + 1) * LANES)
        ka_ref[:, 2 * p * LANES:(2 * p + 1) * LANES] = k_even[:, src]
        ka_ref[:, (2 * p + 1) * LANES:(2 * p + 2) * LANES] = k_odd[:, src]

    parts = _split3(c_t * LOG2E)
    group = 2 * SUBLANES
    r = lax.broadcasted_iota(jnp.int32, (group, tm), 0)
    padding = jnp.zeros((FOX_HEAD_DIM - group, tm), BF16)
    for h in range(FOX_HEADS):
        spare = jnp.where(r < AUG_PARTS, 1.0, 0.0)
        for j, part in enumerate(parts):
            spare = jnp.where(r == AUG_PARTS + j, part[h:h + 1, :], spare)
        feat = slice(h * FOX_HEAD_DIM, (h + 1) * FOX_HEAD_DIM)
        first = slice(h * LANES, h * LANES + FOX_HEAD_DIM)
        second = slice(h * LANES + FOX_HEAD_DIM, (h + 1) * LANES)
        q_rows, spare_rows = (first, second) if h % 2 == 0 else (second, first)
        qa_ref[0, q_rows, :] = q_t[feat, :]
        qa_ref[0, spare_rows, :] = jnp.concatenate([spare.astype(BF16), padding], axis=0)


def _attention_tile(i, qa_ref, ka_ref, v_ref, o_ref, s0_scr, s1_scr, p0_scr, p1_scr, acc_scr, tq, tk, alongside):
    q_t = (qa_ref[0, 0:LANES, :], qa_ref[0, LANES:2 * LANES, :])
    hd = FOX_HEAD_DIM
    heads = range(2)
    s_scr = (s0_scr, s1_scr)
    p_scr = (p0_scr, p1_scr)
    ones_rows = jnp.ones((ATTN_ONES_ROWS, tk), BF16)

    def logits(b, slot):
        start = pl.multiple_of(b * tk, tk)
        for e in heads:
            s_scr[slot][e] = _dot(ka_ref[pl.ds(start, tk), e * LANES:(e + 1) * LANES], q_t[e])

    def softmax(e, slot, m_old, mask):
        s = s_scr[slot][e] if mask is None else jnp.where(mask, s_scr[slot][e], MASK_VALUE)
        m_new = jnp.maximum(m_old[e], jnp.max(s, axis=0, keepdims=True))
        p_scr[slot][e] = jnp.exp2(s - m_new).astype(BF16)
        return m_new, jnp.exp2(m_old[e] - m_new)

    def weighted_values(b, slot, corr):
        start = pl.multiple_of(b * tk, tk)
        for e in heads:
            v_ext = jnp.concatenate([v_ref[0, e * hd:(e + 1) * hd, pl.ds(start, tk)], ones_rows], axis=0)
            acc_scr[e] = corr[e] * acc_scr[e] + _dot(v_ext, p_scr[slot][e])

    def step(b, slot, m_old, corr_prev, mask, last):
        weighted_values(jnp.maximum(b - 1, 0), 1 - slot, corr_prev)
        first = softmax(0, slot, m_old, mask)
        if not last:
            logits(b + 1, 1 - slot)
        second = softmax(1, slot, m_old, mask)
        return tuple(zip(first, second))

    per_q = tq // tk

    def full_blocks(first, count):
        def body(t, carry):
            for d in range(count):
                carry = step(first + count * t + d, d % 2, *carry, None, False)
            return carry
        return body

    logits(0, 0)
    p1_scr[...] = jnp.zeros_like(p1_scr)
    acc_scr[...] = jnp.zeros_like(acc_scr)
    carry = ((jnp.full((1, tq), MASK_VALUE, F32),) * 2, (jnp.ones((1, tq), F32),) * 2)
    n_full = i * per_q
    long_trip = ATTN_UNROLL * per_q
    n_long = n_full // long_trip
    carry = lax.fori_loop(0, n_long, full_blocks(0, long_trip), carry)
    carry = lax.fori_loop(0, (n_full - n_long * long_trip) // per_q, full_blocks(n_long * long_trip, per_q), carry)
    key = lax.broadcasted_iota(jnp.int32, (tk, tq), 0)
    query = lax.broadcasted_iota(jnp.int32, (tk, tq), 1)
    for d in range(per_q):
        carry = step(n_full + d, d % 2, *carry, key + d * tk <= query, d == per_q - 1)
    alongside()
    weighted_values(n_full + per_q - 1, (per_q - 1) % 2, carry[1])
    out_t = jnp.concatenate([acc_scr[e, 0:hd, :] * (1.0 / acc_scr[e, hd:hd + 1, :]) for e in heads], axis=0)
    o_ref[...] = out_t.T.astype(BF16)


def _head_rows():
    sel = np.zeros((FOX_HEADS, D_MODEL), np.float32)
    for h in range(FOX_HEADS):
        sel[h, h * FOX_HEAD_DIM:(h + 1) * FOX_HEAD_DIM] = 1.0
    return sel


def _fox_mix_kernel(pt_ref, qa_ref, ka_ref, v_ref, q_ref, kn_ref, vn_ref, lfn_ref, *refs,
                    tq, tk, group, steps_per_seq):
    ck_refs, cv_refs, clf_refs = refs[0:group], refs[group:2 * group], refs[2 * group:3 * group]
    (sel_ref, later_ref, o_ref, od_ref, s0_scr, s1_scr, p0_scr, p1_scr, attn_acc_scr,
     qb_ref, m_ref, l_ref, acc_ref, suf_ref, snew_ref) = refs[3 * group:]
    step = (pl.program_id(0) * pl.num_programs(1) + pl.program_id(1)) * pl.num_programs(2) + pl.program_id(2)
    j = lax.rem(step, steps_per_seq)
    hd = FOX_HEAD_DIM

    @pl.when(j == 0)
    def _():
        q_row = q_ref[0]
        for c in range(D_MODEL // LANES):
            chunk = slice(c * LANES, (c + 1) * LANES)
            qb_ref[chunk, :] = jnp.broadcast_to(_row_to_col(q_row[:, chunk]), (LANES, PAGE_SIZE))
        s_new = jnp.sum(sel_ref[...] * (q_row * kn_ref[0]), axis=1, keepdims=True)
        snew_ref[...] = s_new
        m_ref[...] = s_new
        l_ref[...] = jnp.ones_like(l_ref)
        acc_ref[...] = jnp.zeros_like(acc_ref)
        suf_ref[...] = lfn_ref[0]

    def decode_pages():
        suf = suf_ref[...]
        bias = []
        for g in range(group):
            lf = clf_refs[g][0]
            bias.append(_dot_tri(lf, later_ref[...]) + suf)
            suf = suf + jnp.sum(lf, axis=1, keepdims=True)
        suf_ref[...] = suf
        bias = jnp.concatenate(bias, axis=1)
        for h in range(FOX_HEADS):
            rows = slice(h * hd, (h + 1) * hd)
            qb = qb_ref[rows, :]
            s = jnp.concatenate([jnp.sum(ck_refs[g][0, rows, :] * qb, axis=0, keepdims=True)
                                 for g in range(group)], axis=1) + bias[h:h + 1, :]
            m_old = m_ref[h:h + 1, :]
            m_new = jnp.maximum(m_old, jnp.max(s, axis=1, keepdims=True))
            corr = jnp.exp(m_old - m_new)
            p = jnp.exp(s - m_new)
            l_ref[h:h + 1, :] = corr * l_ref[h:h + 1, :] + jnp.sum(p, axis=1, keepdims=True)
            m_ref[h:h + 1, :] = m_new
            acc = corr * acc_ref[rows, :]
            for g in range(group):
                acc = acc + cv_refs[g][0, rows, :] * p[:, g * PAGE_SIZE:(g + 1) * PAGE_SIZE]
            acc_ref[rows, :] = acc

    _attention_tile(pl.program_id(2), qa_ref, ka_ref, v_ref, o_ref, s0_scr, s1_scr, p0_scr, p1_scr,
                    attn_acc_scr, tq, tk, decode_pages)

    @pl.when(j == steps_per_seq - 1)
    def _():
        sel = sel_ref[...]
        w_new = jnp.sum(sel * jnp.exp(snew_ref[...] - m_ref[...]), axis=0, keepdims=True)
        inv_l = jnp.sum(sel * (1.0 / l_ref[...]), axis=0, keepdims=True)
        eye = (lax.broadcasted_iota(jnp.int32, (LANES, LANES), 0)
               == lax.broadcasted_iota(jnp.int32, (LANES, LANES), 1))
        past = []
        for c in range(D_MODEL // LANES):
            col = jnp.sum(acc_ref[c * LANES:(c + 1) * LANES, :], axis=1, keepdims=True)
            past.append(jnp.sum(jnp.where(eye, col, 0.0), axis=0, keepdims=True))
        out = (jnp.concatenate(past, axis=1) + w_new * vn_ref[0]) * inv_l
        od_ref[0] = out.astype(BF16)


def _fox_mix(qa, ka, vtb, batch, seq, tq, tk, q, k_new, v_new, lf_new, cache_k, cache_v, cache_lf, page_table):
    nq = seq // tq
    assert (tq // tk) % 2 == 0, "key blocks alternate between two scratch slots"
    nb, n_pages = page_table.shape
    n_pool = cache_k.shape[0]
    n_steps = batch * FOX_PAIRS * nq
    group, rest = divmod(nb * n_pages, n_steps)
    assert rest == 0 and group > 0 and n_pages % group == 0, "cache pages must spread evenly over the grid steps"
    steps_per_seq = n_pages // group
    later = jnp.asarray(np.tril(np.ones((PAGE_SIZE, PAGE_SIZE), np.float32), -1), BF16)
    kt = jnp.transpose(cache_k, (0, 2, 3, 1)).reshape(n_pool, D_MODEL, PAGE_SIZE)
    vt = jnp.transpose(cache_v, (0, 2, 3, 1)).reshape(n_pool, D_MODEL, PAGE_SIZE)
    lft = jnp.transpose(cache_lf, (0, 2, 1))

    def step_of(b, p, i):
        return (b * FOX_PAIRS + p) * nq + i

    def page(g):
        def index(b, p, i, pt):
            s = step_of(b, p, i)
            return (pt[s // steps_per_seq, n_pages - 1 - ((s % steps_per_seq) * group + g)], 0, 0)
        return index

    def pages(rows):
        return [pl.BlockSpec((1, rows, PAGE_SIZE), page(g)) for g in range(group)]

    whole = lambda shape: pl.BlockSpec(shape, lambda b, p, i, pt: (0,) * len(shape))
    per_seq = lambda *shape: pl.BlockSpec((1,) + shape, lambda b, p, i, pt: (step_of(b, p, i) // steps_per_seq, 0, 0))
    grid_spec = pltpu.PrefetchScalarGridSpec(
        num_scalar_prefetch=1,
        grid=(batch, FOX_PAIRS, nq),
        in_specs=[pl.BlockSpec((1, 2 * LANES, tq), lambda b, p, i, pt: (b, p, i)),
                  pl.BlockSpec((seq, 2 * LANES), lambda b, p, i, pt: (b, p)),
                  pl.BlockSpec((1, LANES, seq), lambda b, p, i, pt: (b, p, 0)),
                  per_seq(1, D_MODEL), per_seq(1, D_MODEL), per_seq(1, D_MODEL), per_seq(FOX_HEADS, 1),
                  *pages(D_MODEL), *pages(D_MODEL), *pages(FOX_HEADS),
                  whole((FOX_HEADS, D_MODEL)), whole((PAGE_SIZE, PAGE_SIZE))],
        out_specs=[pl.BlockSpec((tq, LANES), lambda b, p, i, pt: (b * nq + i, p)), per_seq(1, D_MODEL)],
        scratch_shapes=[pltpu.VMEM((2, tk, tq), F32), pltpu.VMEM((2, tk, tq), F32),
                        pltpu.VMEM((2, tk, tq), BF16), pltpu.VMEM((2, tk, tq), BF16),
                        pltpu.VMEM((2, FOX_HEAD_DIM + ATTN_ONES_ROWS, tq), F32),
                        pltpu.VMEM((D_MODEL, PAGE_SIZE), F32),
                        pltpu.VMEM((FOX_HEADS, 1), F32), pltpu.VMEM((FOX_HEADS, 1), F32),
                        pltpu.VMEM((D_MODEL, PAGE_SIZE), F32),
                        pltpu.VMEM((FOX_HEADS, 1), F32), pltpu.VMEM((FOX_HEADS, 1), F32)],
    )
    row = lambda a: a.astype(F32).reshape(nb, 1, -1)
    attn, out = pl.pallas_call(
        functools.partial(_fox_mix_kernel, tq=tq, tk=tk, group=group, steps_per_seq=steps_per_seq),
        grid_spec=grid_spec,
        out_shape=[jax.ShapeDtypeStruct((batch * seq, D_MODEL), BF16),
                   jax.ShapeDtypeStruct((nb, 1, D_MODEL), BF16)],
        compiler_params=_params("arbitrary", "arbitrary", "arbitrary"),
        name="fox_mix",
    )(page_table, qa, ka, vtb, row(q), row(k_new), row(v_new), lf_new.reshape(nb, FOX_HEADS, 1),
      *([kt] * group), *([vt] * group), *([lft] * group), jnp.asarray(_head_rows()), later)
    return attn, out.reshape(nb, D_MODEL)


def _gla_proj_kernel(x_ref, w_ref, wa1_ref, wa2_ref, ba_ref, q_ref, k_ref, v_ref, r_ref, la_ref):
    xb = x_ref[...].astype(BF16)
    q_ref[...] = _dot(xb, w_ref[:, 0:GLA_DK])
    k_ref[...] = _dot(xb, w_ref[:, GLA_DK:2 * GLA_DK])
    v_ref[...] = _dot(xb, w_ref[:, 2 * GLA_DK:2 * GLA_DK + GLA_DV]).astype(BF16)
    r_ref[...] = _dot(xb, w_ref[:, 2 * GLA_DK + GLA_DV:2 * GLA_DK + 2 * GLA_DV])
    a_lr = _dot(xb, wa1_ref[...]).astype(BF16)
    la_ref[...] = _log_sigmoid(_dot(a_lr, wa2_ref[...]) + ba_ref[...]) / GLA_TAU


def _gla_proj(x, w, w_a1, w_a2, b_a, tm):
    n = x.shape[0]
    row = pl.BlockSpec((tm, D_MODEL), lambda i: (i, 0))
    half = pl.BlockSpec((tm, GLA_DK), lambda i: (i, 0))
    return pl.pallas_call(
        _gla_proj_kernel,
        grid=(n // tm,),
        in_specs=[row, _resident(w.shape), _resident(w_a1.shape), _resident(w_a2.shape), _resident((1, GLA_DK))],
        out_specs=[half, half, row, row, half],
        out_shape=[jax.ShapeDtypeStruct((n, GLA_DK), F32), jax.ShapeDtypeStruct((n, GLA_DK), F32),
                   jax.ShapeDtypeStruct((n, GLA_DV), BF16), jax.ShapeDtypeStruct((n, GLA_DV), F32),
                   jax.ShapeDtypeStruct((n, GLA_DK), F32)],
        compiler_params=_params("parallel"),
        name="gla_proj",
    )(x, w, w_a1, w_a2, b_a)


def _row_to_col(row):
    n = row.shape[1]
    eye = lax.broadcasted_iota(jnp.int32, (n, n), 0) == lax.broadcasted_iota(jnp.int32, (n, n), 1)
    return jnp.sum(jnp.where(eye, row, 0.0), axis=1, keepdims=True)


def _gla_chunk_kernel(q_ref, k_ref, v_ref, la_ref, tri_ref, o_ref, s_out_ref, s_ref, *, n_chunks):
    t = pl.program_id(2)

    @pl.when(t == 0)
    def _():
        s_ref[...] = jnp.zeros_like(s_ref)

    c = GLA_CHUNK
    ts = n_chunks * c
    chunk_rows = [slice(n * c, (n + 1) * c) for n in range(n_chunks)]
    bc = jnp.concatenate([_tri_dot(tri_ref[...], la_ref[r, :]) for r in chunk_rows], axis=0)
    b_last = [bc[r.stop - 1:r.stop, :] for r in chunk_rows]
    b_last_rows = jnp.concatenate([jnp.broadcast_to(b, (c, b.shape[1])) for b in b_last], axis=0)
    k = k_ref[...]
    v = v_ref[...]
    q_dec = (q_ref[...] * GLA_DK_H ** -0.5 * jnp.exp(bc)).astype(BF16)
    k_inv = (k * jnp.exp(-bc)).astype(BF16)
    k_end = (k * jnp.exp(b_last_rows - bc)).astype(BF16)
    row = lax.broadcasted_iota(jnp.int32, (ts, ts), 0)
    col = lax.broadcasted_iota(jnp.int32, (ts, ts), 1)
    same_chunk_causal = (col <= row) & (col >= (row & -c))
    attn = jnp.where(same_chunk_causal, _dot_nt(q_dec, k_inv), 0.0).astype(BF16)
    o_intra = _dot(attn, v)
    kv = [_dot_tn(k_end[r, :], v[r, :]) for r in chunk_rows]
    decay = [_row_to_col(jnp.exp(b)) for b in b_last]
    s = s_ref[...]
    s_before = []
    for n in range(n_chunks):
        s_before.append(s.astype(BF16))
        s = decay[n] * s + kv[n]
    s_ref[...] = s
    for n, r in enumerate(chunk_rows):
        o_ref[r, :] = o_intra[r, :] + _dot(q_dec[r, :], s_before[n])

    @pl.when(t == pl.num_programs(2) - 1)
    def _():
        s_out_ref[0, 0] = s


def _gla_chunked(q, k, v, la, batch, seq, ts):
    nt = seq // ts
    n_chunks = ts // GLA_CHUNK
    tri = jnp.asarray(np.tril(np.ones((GLA_CHUNK, GLA_CHUNK), np.float32)), BF16)
    key = pl.BlockSpec((ts, GLA_DK_H), lambda b, h, t: (b * nt + t, h))
    val = pl.BlockSpec((ts, GLA_DV_H), lambda b, h, t: (b * nt + t, h))
    return pl.pallas_call(
        functools.partial(_gla_chunk_kernel, n_chunks=n_chunks),
        grid=(batch, GLA_HEADS, nt),
        in_specs=[key, key, val, key, _resident((GLA_CHUNK, GLA_CHUNK))],
        out_specs=[val, pl.BlockSpec((1, 1, GLA_DK_H, GLA_DV_H), lambda b, h, t: (b, h, 0, 0))],
        out_shape=[jax.ShapeDtypeStruct((batch * seq, GLA_DV), F32),
                   jax.ShapeDtypeStruct((batch, GLA_HEADS, GLA_DK_H, GLA_DV_H), F32)],
        scratch_shapes=[pltpu.VMEM((GLA_DK_H, GLA_DV_H), F32)],
        compiler_params=_params("parallel", "parallel", "arbitrary"),
        name="gla_chunked",
    )(q, k, v, la, tri)


def _gla_step_kernel(q_ref, k_ref, v_ref, la_ref, s0_ref, o_ref, s_ref):
    for h in range(GLA_HEADS):
        la = la_ref[0, h]
        k = k_ref[0, h]
        v = v_ref[0, h]
        s0 = s0_ref[0, h]
        decay = jnp.exp(la)
        q_dec = (q_ref[0, h] * GLA_DK_H ** -0.5 * decay).astype(BF16)
        k_inv = (k * jnp.exp(-la)).astype(BF16)
        attn = jnp.sum(q_dec.astype(F32) * k_inv.astype(F32), axis=1, keepdims=True).astype(BF16).astype(F32)
        q_rows = jnp.broadcast_to(q_dec.astype(F32), (SUBLANES, GLA_DK_H)).astype(BF16)
        o_ref[0, h] = attn * v + _dot(q_rows, s0.astype(BF16))[0:1, :]
        s_ref[0, h] = _row_to_col(decay) * s0 + _row_to_col(k.astype(BF16).astype(F32)) * v


def _gla_step(q, k, v, la, s0):
    nb = q.shape[0]
    heads = lambda a, width: a.astype(F32).reshape(nb, GLA_HEADS, 1, width)
    key = pl.BlockSpec((1, GLA_HEADS, 1, GLA_DK_H), lambda b: (b, 0, 0, 0))
    val = pl.BlockSpec((1, GLA_HEADS, 1, GLA_DV_H), lambda b: (b, 0, 0, 0))
    state = pl.BlockSpec((1, GLA_HEADS, GLA_DK_H, GLA_DV_H), lambda b: (b, 0, 0, 0))
    o, s = pl.pallas_call(
        _gla_step_kernel,
        grid=(nb,),
        in_specs=[key, key, val, key, state],
        out_specs=[val, state],
        out_shape=[jax.ShapeDtypeStruct((nb, GLA_HEADS, 1, GLA_DV_H), F32),
                   jax.ShapeDtypeStruct((nb, GLA_HEADS, GLA_DK_H, GLA_DV_H), F32)],
        compiler_params=_params("parallel"),
        name="gla_step",
    )(heads(q, GLA_DK_H), heads(k, GLA_DK_H), heads(v, GLA_DV_H), heads(la, GLA_DK_H), s0)
    return o.reshape(nb, GLA_DV), s


PROMPT_TILE = 512
WIDE_TILE = 1024
ATTN_Q_TILE = 512
ATTN_K_TILE = 256
GLA_TILE = 512


def kernel(x_prompt, x_sample, cache_fox_k, cache_fox_v, cache_fox_logf, state_gla, page_table,
           ln_g, ln_b, ffn_w_in, ffn_w_out, fox_w_in, fox_b_f, fox_w_o,
           gla_w_in, gla_w_a2, gla_b_a, gla_norm_g, gla_w_o):
    batch, seq, _ = x_prompt.shape
    nb = x_sample.shape[0]
    xp = x_prompt.reshape(batch * seq, D_MODEL)
    xs = x_sample.reshape(nb, D_MODEL)
    tp = min(PROMPT_TILE, batch * seq)
    tw = min(WIDE_TILE, batch * seq)
    vec = lambda a: a.reshape(1, -1)
    w_in = ffn_w_in.astype(BF16)
    w_out = ffn_w_out.astype(BF16)

    def ffn(i, half, ln):
        return ((w_in, (i, half)), (w_out, (i, half)), vec(ln_g[i, ln]), vec(ln_b[i, ln]))

    xp, xs = _ffn_half(xp, xs, *ffn(0, 0, 0), tw)
    w_fox = fox_w_in[0].astype(BF16)
    w_qkv, w_f = w_fox[:, :3 * D_MODEL], w_fox[:, 3 * D_MODEL:]
    kt, vt, vtb, lft, qa, ka = _fox_proj_t(xp, w_fox.T, w_fox[:, D_MODEL:2 * D_MODEL], w_f,
                                           vec(fox_b_f[0]), batch, seq, min(PROMPT_TILE, seq))
    qs, ks, vs, lfs = _fox_proj(xs, w_qkv, w_f, vec(fox_b_f[0]), nb)
    mp, ms = _fox_mix(qa, ka, vtb, batch, seq, min(ATTN_Q_TILE, seq), min(ATTN_K_TILE, seq),
                      qs, ks, vs, lfs, cache_fox_k[0], cache_fox_v[0], cache_fox_logf[0], page_table)
    mix = (fox_w_o[0].astype(BF16), vec(ln_g[0, 1]), vec(ln_b[0, 1])) + ffn(0, 1, 2)
    xp, xs = _mix_ffn(xp, mp, xs, ms, *mix, tp)

    xp, xs = _ffn_half(xp, xs, *ffn(1, 0, 0), tw)
    w_gla = gla_w_in[0].astype(BF16)
    n_main = 2 * GLA_DK + 2 * GLA_DV
    proj = (w_gla[:, :n_main], w_gla[:, n_main:], gla_w_a2[0].astype(BF16), vec(gla_b_a[0]))
    gqp, gkp, gvp, grp, glap = _gla_proj(xp, *proj, tw)
    gqs, gks, gvs, grs, glas = _gla_proj(xs, *proj, nb)
    op, state_p = _gla_chunked(gqp, gkp, gvp, glap, batch, seq, min(GLA_TILE, seq))
    os_, state_s = _gla_step(gqs, gks, gvs, glas, state_gla[0])
    mix = (vec(gla_norm_g[0]), gla_w_o[0].astype(BF16), vec(ln_g[1, 1]), vec(ln_b[1, 1])) + ffn(1, 1, 2)
    xp, xs = _gla_mix_ffn(xp, op, grp, xs, os_, grs, *mix, tp)

    heads = (FOX_HEADS, FOX_HEAD_DIM)
    token_major = lambda a: jnp.transpose(a.reshape(batch, *heads, seq), (0, 3, 1, 2))[None]
    return (xp.reshape(batch, seq, D_MODEL), xs.reshape(nb, 1, D_MODEL),
            token_major(kt), token_major(vt), jnp.transpose(lft, (0, 2, 1))[None],
            ks.reshape(1, nb, 1, *heads), vs.reshape(1, nb, 1, *heads), lfs.reshape(1, nb, 1, FOX_HEADS),
            state_p[None], state_s[None])
```

```python
import functools

import numpy as np
import jax
import jax.numpy as jnp
from jax import lax
from jax.experimental import pallas as pl
from jax.experimental.pallas import tpu as pltpu

F32 = jnp.float32
BF16 = jnp.bfloat16

D_MODEL = 1024
DEPTH = 2
PAGE_SIZE = 128
FOX_HEADS = 16
FOX_HEAD_DIM = D_MODEL // FOX_HEADS
FOX_PAIRS = FOX_HEADS // 2
GLA_HEADS = 4
GLA_DK = D_MODEL // 2
GLA_DV = D_MODEL
GLA_DK_H = GLA_DK // GLA_HEADS
GLA_DV_H = GLA_DV // GLA_HEADS
GLA_TAU = 16.0
GLA_CHUNK = 64
D_FF = ((8 * D_MODEL // 3 + 127) // 128) * 128
DN_ALPHA = (2 * DEPTH) ** 0.25
LN_EPS = 1e-5
MASK_VALUE = -1e30
LOG2E = 1.4426950408889634

LANES = 128
SUBLANES = 8
MXU_WIDTH = 256
FF_CHUNK = MXU_WIDTH
AUG_PARTS = 3
VMEM_LIMIT = 56 * 2 ** 20
ATTN_ONES_ROWS = 16
ATTN_UNROLL = 4


def _params(*sem):
    return pltpu.CompilerParams(dimension_semantics=sem, vmem_limit_bytes=VMEM_LIMIT)


def _resident(shape):
    return pl.BlockSpec(shape, lambda *_: (0,) * len(shape), pipeline_mode=pl.Buffered(1))


def _stacked_spec(stacked):
    w, index = stacked
    tail = w.shape[len(index):]
    return pl.BlockSpec((None,) * len(index) + tail, lambda *_: index + (0,) * len(tail),
                        pipeline_mode=pl.Buffered(1))


def _dot(a, b):
    return jnp.dot(a, b, preferred_element_type=F32)


def _dot_nt(a, b):
    return lax.dot_general(a, b, (((1,), (1,)), ((), ())), preferred_element_type=F32)


def _dot_tn(a, b):
    return lax.dot_general(a, b, (((0,), (0,)), ((), ())), preferred_element_type=F32)


def _layer_norm(y, g, b):
    mu = jnp.mean(y, axis=-1, keepdims=True)
    d = y - mu
    var = jnp.mean(d * d, axis=-1, keepdims=True)
    return d * lax.rsqrt(var + LN_EPS) * g + b


def _log_sigmoid(z):
    return jnp.minimum(z, 0.0) - jnp.log1p(jnp.exp(-jnp.abs(z)))


def _split3(x):
    hi = x.astype(BF16).astype(F32)
    r = x - hi
    mid = r.astype(BF16).astype(F32)
    lo = (r - mid).astype(BF16).astype(F32)
    return hi, mid, lo


def _tri_dot(tri, x):
    hi, mid, lo = _split3(x)
    return (_dot(tri, hi.astype(BF16)) + _dot(tri, mid.astype(BF16))) + _dot(tri, lo.astype(BF16))


def _dot_tri(x, tri):
    hi, mid, lo = _split3(x)
    return (_dot(hi.astype(BF16), tri) + _dot(mid.astype(BF16), tri)) + _dot(lo.astype(BF16), tri)


def _swiglu_ln(x, win_ref, wout_ref, g, b):
    xb = x.astype(BF16)
    acc = jnp.zeros(x.shape, F32)
    for c in range(D_FF // FF_CHUNK):
        lo = c * FF_CHUNK
        gate = _dot(xb, win_ref[:, lo:lo + FF_CHUNK])
        up = _dot(xb, win_ref[:, D_FF + lo:D_FF + lo + FF_CHUNK])
        h = (gate * jax.nn.sigmoid(gate) * up).astype(BF16)
        acc = acc + _dot(h, wout_ref[lo:lo + FF_CHUNK, :])
    return _layer_norm(DN_ALPHA * x + 0.5 * acc, g, b)


def _on_last_step(fn):
    pl.when(pl.program_id(0) == pl.num_programs(0) - 1)(fn)


def _sample_rows(a):
    return pl.BlockSpec(a.shape, lambda i: (0,) * len(a.shape))


def _ffn_kernel(x_ref, xs_ref, win_ref, wout_ref, g_ref, b_ref, o_ref, os_ref):
    def rows(x_ref, o_ref):
        o_ref[...] = _swiglu_ln(x_ref[...], win_ref, wout_ref, g_ref[...], b_ref[...])

    rows(x_ref, o_ref)
    _on_last_step(lambda: rows(xs_ref, os_ref))


def _ffn_half(x, xs, w_in, w_out, g, b, tm):
    n = x.shape[0]
    row = pl.BlockSpec((tm, D_MODEL), lambda i: (i, 0))
    return pl.pallas_call(
        _ffn_kernel,
        grid=(n // tm,),
        in_specs=[row, _sample_rows(xs), _stacked_spec(w_in), _stacked_spec(w_out),
                  _resident((1, D_MODEL)), _resident((1, D_MODEL))],
        out_specs=[row, _sample_rows(xs)],
        out_shape=[jax.ShapeDtypeStruct((n, D_MODEL), F32), jax.ShapeDtypeStruct(xs.shape, F32)],
        compiler_params=_params("arbitrary"),
        name="ffn_half",
    )(x, xs, w_in[0], w_out[0], g, b)


def _mix_ffn_kernel(x_ref, m_ref, xs_ref, ms_ref, wo_ref, g1_ref, b1_ref, win_ref, wout_ref, g2_ref, b2_ref,
                    o_ref, os_ref):
    def rows(x_ref, m_ref, o_ref):
        x1 = _layer_norm(DN_ALPHA * x_ref[...] + _dot(m_ref[...], wo_ref[...]), g1_ref[...], b1_ref[...])
        o_ref[...] = _swiglu_ln(x1, win_ref, wout_ref, g2_ref[...], b2_ref[...])

    rows(x_ref, m_ref, o_ref)
    _on_last_step(lambda: rows(xs_ref, ms_ref, os_ref))


def _gla_mix_ffn_kernel(x_ref, att_ref, r_ref, xs_ref, atts_ref, rs_ref, gn_ref, wo_ref, g1_ref, b1_ref,
                        win_ref, wout_ref, g2_ref, b2_ref, o_ref, os_ref):
    refs = (gn_ref, wo_ref, g1_ref, b1_ref, win_ref, wout_ref, g2_ref, b2_ref)
    _gla_mix_rows(x_ref, att_ref, r_ref, *refs, o_ref)
    _on_last_step(lambda: _gla_mix_rows(xs_ref, atts_ref, rs_ref, *refs, os_ref))


def _gla_mix_rows(x_ref, att_ref, r_ref, gn_ref, wo_ref, g1_ref, b1_ref, win_ref, wout_ref, g2_ref, b2_ref, o_ref):
    heads = []
    for h in range(GLA_HEADS):
        o = att_ref[:, h * GLA_DV_H:(h + 1) * GLA_DV_H]
        mu = jnp.mean(o, axis=-1, keepdims=True)
        d = o - mu
        var = jnp.mean(d * d, axis=-1, keepdims=True)
        heads.append(d * lax.rsqrt(var + LN_EPS))
    on = jnp.concatenate(heads, axis=-1) * gn_ref[...]
    r = r_ref[...]
    m = (on * (r * jax.nn.sigmoid(r))).astype(BF16)
    x1 = _layer_norm(DN_ALPHA * x_ref[...] + _dot(m, wo_ref[...]), g1_ref[...], b1_ref[...])
    o_ref[...] = _swiglu_ln(x1, win_ref, wout_ref, g2_ref[...], b2_ref[...])


def _mix_ffn(x, m, xs, ms, w_o, g1, b1, w_in, w_out, g2, b2, tm):
    n = x.shape[0]
    row = pl.BlockSpec((tm, D_MODEL), lambda i: (i, 0))
    vec = _resident((1, D_MODEL))
    return pl.pallas_call(
        _mix_ffn_kernel,
        grid=(n // tm,),
        in_specs=[row, row, _sample_rows(xs), _sample_rows(ms), _resident(w_o.shape), vec, vec,
                  _stacked_spec(w_in), _stacked_spec(w_out), vec, vec],
        out_specs=[row, _sample_rows(xs)],
        out_shape=[jax.ShapeDtypeStruct((n, D_MODEL), F32), jax.ShapeDtypeStruct(xs.shape, F32)],
        compiler_params=_params("arbitrary"),
        name="mix_ffn",
    )(x, m, xs, ms, w_o, g1, b1, w_in[0], w_out[0], g2, b2)


def _gla_mix_ffn(x, o, r, xs, os_, rs, gn, w_o, g1, b1, w_in, w_out, g2, b2, tm):
    n = x.shape[0]
    row = pl.BlockSpec((tm, D_MODEL), lambda i: (i, 0))
    vec = _resident((1, D_MODEL))
    return pl.pallas_call(
        _gla_mix_ffn_kernel,
        grid=(n // tm,),
        in_specs=[row, row, row, _sample_rows(xs), _sample_rows(os_), _sample_rows(rs),
                  vec, _resident(w_o.shape), vec, vec,
                  _stacked_spec(w_in), _stacked_spec(w_out), vec, vec],
        out_specs=[row, _sample_rows(xs)],
        out_shape=[jax.ShapeDtypeStruct((n, D_MODEL), F32), jax.ShapeDtypeStruct(xs.shape, F32)],
        compiler_params=_params("arbitrary"),
        name="gla_mix_ffn",
    )(x, o, r, xs, os_, rs, gn, w_o, g1, b1, w_in[0], w_out[0], g2, b2)


def _fox_proj_kernel(x_ref, wt_ref, bf_ref, q_ref, k_ref, v_ref, lf_ref):
    xb = x_ref[...].astype(BF16)
    q = _dot_nt(xb, wt_ref[0:D_MODEL, :])
    q_ref[...] = (q * FOX_HEAD_DIM ** -0.5).astype(BF16)
    k_ref[...] = _dot_nt(xb, wt_ref[D_MODEL:2 * D_MODEL, :])
    v_ref[...] = _dot_nt(xb, wt_ref[2 * D_MODEL:3 * D_MODEL, :])
    lf_ref[...] = _log_sigmoid(_dot_nt(xb, wt_ref[3 * D_MODEL:3 * D_MODEL + FOX_HEADS, :]) + bf_ref[...])


def _fox_proj(x, w_t, b_f, tm):
    n = x.shape[0]
    row = pl.BlockSpec((tm, D_MODEL), lambda i: (i, 0))
    gate = pl.BlockSpec((tm, FOX_HEADS), lambda i: (i, 0))
    full = lambda dt: jax.ShapeDtypeStruct((n, D_MODEL), dt)
    return pl.pallas_call(
        _fox_proj_kernel,
        grid=(n // tm,),
        in_specs=[row, _resident(w_t.shape), _resident((1, FOX_HEADS))],
        out_specs=[row, row, row, gate],
        out_shape=[full(BF16), full(F32), full(F32), jax.ShapeDtypeStruct((n, FOX_HEADS), F32)],
        compiler_params=_params("parallel"),
        name="fox_proj",
    )(x, w_t, b_f)


def _fox_proj_t_kernel(x_ref, wt_ref, bf_ref, bft_ref, tril_ref, triu_ref, pk_ref, ok_ref,
                       kt_ref, vt_ref, vtb_ref, lft_ref, qa_ref, ka_ref, carry_row_ref, carry_col_ref):
    xb = x_ref[...].astype(BF16)
    q_t = (_dot_nt(wt_ref[0:D_MODEL, :], xb) * (FOX_HEAD_DIM ** -0.5 * LOG2E)).astype(BF16)
    kt_ref[0] = _dot_nt(wt_ref[D_MODEL:2 * D_MODEL, :], xb)
    k = _dot_nt(xb, wt_ref[D_MODEL:2 * D_MODEL, :])
    v_t = _dot_nt(wt_ref[2 * D_MODEL:3 * D_MODEL, :], xb)
    vt_ref[0] = v_t
    vtb_ref[0] = v_t.astype(BF16)
    w_gate = wt_ref[3 * D_MODEL:3 * D_MODEL + FOX_HEADS, :]
    lf = _log_sigmoid(_dot_nt(xb, w_gate) + bf_ref[...])
    lf_t = _log_sigmoid(_dot_nt(w_gate, xb) + bft_ref[...])
    lft_ref[0] = lf_t
    _write_augmented(q_t, k, lf, lf_t, tril_ref, triu_ref, pk_ref, ok_ref, qa_ref, ka_ref,
                     carry_row_ref, carry_col_ref)


def _fox_proj_t(x, w_t, b_f, batch, seq, tm):
    pk, ones_k = _aug_placement()
    tril = jnp.asarray(np.tril(np.ones((tm, tm), np.float32)), BF16)
    triu = jnp.asarray(np.triu(np.ones((tm, tm), np.float32)), BF16)
    nt = seq // tm
    row = pl.BlockSpec((tm, D_MODEL), lambda b, t: (b * nt + t, 0))
    wide = pl.BlockSpec((tm, 2 * D_MODEL), lambda b, t: (b * nt + t, 0))
    col = pl.BlockSpec((1, D_MODEL, tm), lambda b, t: (b, 0, t))
    tall = pl.BlockSpec((1, 2 * D_MODEL, tm), lambda b, t: (b, 0, t))
    gate_t = pl.BlockSpec((1, FOX_HEADS, tm), lambda b, t: (b, 0, t))
    feat = lambda dt: jax.ShapeDtypeStruct((batch, D_MODEL, seq), dt)
    return pl.pallas_call(
        _fox_proj_t_kernel,
        grid=(batch, nt),
        in_specs=[row, _resident(w_t.shape), _resident((1, FOX_HEADS)), _resident((FOX_HEADS, 1)),
                  _resident((tm, tm)), _resident((tm, tm)), _resident(pk.shape), _resident((1, D_MODEL))],
        out_specs=[col, col, col, gate_t, tall, wide],
        out_shape=[feat(F32), feat(F32), feat(BF16), jax.ShapeDtypeStruct((batch, FOX_HEADS, seq), F32),
                   jax.ShapeDtypeStruct((batch, 2 * D_MODEL, seq), BF16),
                   jax.ShapeDtypeStruct((batch * seq, 2 * D_MODEL), BF16)],
        scratch_shapes=[pltpu.VMEM((1, FOX_HEADS), F32), pltpu.VMEM((FOX_HEADS, 1), F32)],
        compiler_params=_params("parallel", "arbitrary"),
        name="fox_proj_t",
    )(x, w_t, b_f, b_f.reshape(FOX_HEADS, 1), tril, triu, jnp.asarray(pk, BF16), jnp.asarray(ones_k))


def _aug_placement():
    pk = np.zeros((AUG_PARTS, FOX_HEADS, D_MODEL), np.float32)
    ones_k = np.zeros((1, D_MODEL), np.float32)
    for h in range(FOX_HEADS):
        base = (h // 2) * LANES + (FOX_HEAD_DIM if h % 2 == 0 else 0)
        for j in range(AUG_PARTS):
            pk[j, h, base + j] = -1.0
            ones_k[0, base + AUG_PARTS + j] = 1.0
    return pk, ones_k


def _write_augmented(q_t, k, lf, lf_t, tril_ref, triu_ref, pk_ref, ok_ref,
                     qa_ref, ka_ref, carry_row_ref, carry_col_ref):
    @pl.when(pl.program_id(1) == 0)
    def _():
        carry_row_ref[...] = jnp.zeros_like(carry_row_ref)
        carry_col_ref[...] = jnp.zeros_like(carry_col_ref)

    tm = k.shape[0]
    c = _tri_dot(tril_ref[...], lf) + carry_row_ref[...]
    c_t = _dot_tri(lf_t, triu_ref[...]) + carry_col_ref[...]
    carry_row_ref[...] = c[tm - 1:tm, :]
    carry_col_ref[...] = c_t[:, tm - 1:tm]

    aug_k = ok_ref[...]
    for j, part in enumerate(_split3(c * LOG2E)):
        aug_k = aug_k + _dot(part.astype(BF16), pk_ref[j])
    lane = lax.broadcasted_iota(jnp.int32, (tm, D_MODEL), 1) & (LANES - 1)
    low = lane < FOX_HEAD_DIM
    k_even = jnp.where(low, k, aug_k).astype(BF16)
    k_odd = jnp.where(low, aug_k, k).astype(BF16)
    for p in range(FOX_PAIRS):
        src = slice(p * LANES, (p + 1) * LANES)
        ka_ref[:, 2 * p * LANES:(2 * p + 1) * LANES] = k_even[:, src]
        ka_ref[:, (2 * p + 1) * LANES:(2 * p + 2) * LANES] = k_odd[:, src]

    parts = _split3(c_t * LOG2E)
    group = 2 * SUBLANES
    r = lax.broadcasted_iota(jnp.int32, (group, tm), 0)
    padding = jnp.zeros((FOX_HEAD_DIM - group, tm), BF16)
    for h in range(FOX_HEADS):
        spare = jnp.where(r < AUG_PARTS, 1.0, 0.0)
        for j, part in enumerate(parts):
            spare = jnp.where(r == AUG_PARTS + j, part[h:h + 1, :], spare)
        feat = slice(h * FOX_HEAD_DIM, (h + 1) * FOX_HEAD_DIM)
        first = slice(h * LANES, h * LANES + FOX_HEAD_DIM)
        second = slice(h * LANES + FOX_HEAD_DIM, (h + 1) * LANES)
        q_rows, spare_rows = (first, second) if h % 2 == 0 else (second, first)
        qa_ref[0, q_rows, :] = q_t[feat, :]
        qa_ref[0, spare_rows, :] = jnp.concatenate([spare.astype(BF16), padding], axis=0)


def _attention_tile(i, qa_ref, ka_ref, v_ref, o_ref, s0_scr, s1_scr, p0_scr, p1_scr, acc_scr, tq, tk, alongside):
    q_t = (qa_ref[0, 0:LANES, :], qa_ref[0, LANES:2 * LANES, :])
    hd = FOX_HEAD_DIM
    heads = range(2)
    s_scr = (s0_scr, s1_scr)
    p_scr = (p0_scr, p1_scr)
    ones_rows = jnp.ones((ATTN_ONES_ROWS, tk), BF16)

    def logits(b, slot):
        start = pl.multiple_of(b * tk, tk)
        for e in heads:
            s_scr[slot][e] = _dot(ka_ref[pl.ds(start, tk), e * LANES:(e + 1) * LANES], q_t[e])

    def softmax(e, slot, m_old, mask):
        s = s_scr[slot][e] if mask is None else jnp.where(mask, s_scr[slot][e], MASK_VALUE)
        m_new = jnp.maximum(m_old[e], jnp.max(s, axis=0, keepdims=True))
        p_scr[slot][e] = jnp.exp2(s - m_new).astype(BF16)
        return m_new, jnp.exp2(m_old[e] - m_new)

    def weighted_values(b, slot, corr):
        start = pl.multiple_of(b * tk, tk)
        for e in heads:
            v_ext = jnp.concatenate([v_ref[0, e * hd:(e + 1) * hd, pl.ds(start, tk)], ones_rows], axis=0)
            acc_scr[e] = corr[e] * acc_scr[e] + _dot(v_ext, p_scr[slot][e])

    def step(b, slot, m_old, corr_prev, mask, last):
        weighted_values(jnp.maximum(b - 1, 0), 1 - slot, corr_prev)
        first = softmax(0, slot, m_old, mask)
        if not last:
            logits(b + 1, 1 - slot)
        second = softmax(1, slot, m_old, mask)
        return tuple(zip(first, second))

    per_q = tq // tk

    def full_blocks(first, count):
        def body(t, carry):
            for d in range(count):
                carry = step(first + count * t + d, d % 2, *carry, None, False)
            return carry
        return body

    logits(0, 0)
    p1_scr[...] = jnp.zeros_like(p1_scr)
    acc_scr[...] = jnp.zeros_like(acc_scr)
    carry = ((jnp.full((1, tq), MASK_VALUE, F32),) * 2, (jnp.ones((1, tq), F32),) * 2)
    n_full = i * per_q
    long_trip = ATTN_UNROLL * per_q
    n_long = n_full // long_trip
    carry = lax.fori_loop(0, n_long, full_blocks(0, long_trip), carry)
    carry = lax.fori_loop(0, (n_full - n_long * long_trip) // per_q, full_blocks(n_long * long_trip, per_q), carry)
    key = lax.broadcasted_iota(jnp.int32, (tk, tq), 0)
    query = lax.broadcasted_iota(jnp.int32, (tk, tq), 1)
    for d in range(per_q):
        carry = step(n_full + d, d % 2, *carry, key + d * tk <= query, d == per_q - 1)
    alongside()
    weighted_values(n_full + per_q - 1, (per_q - 1) % 2, carry[1])
    out_t = jnp.concatenate([acc_scr[e, 0:hd, :] * (1.0 / acc_scr[e, hd:hd + 1, :]) for e in heads], axis=0)
    o_ref[...] = out_t.T.astype(BF16)


def _head_rows():
    sel = np.zeros((FOX_HEADS, D_MODEL), np.float32)
    for h in range(FOX_HEADS):
        sel[h, h * FOX_HEAD_DIM:(h + 1) * FOX_HEAD_DIM] = 1.0
    return sel


def _fox_mix_kernel(pt_ref, qa_ref, ka_ref, v_ref, q_ref, kn_ref, vn_ref, lfn_ref, *refs,
                    tq, tk, group, steps_per_seq):
    ck_refs, cv_refs, clf_refs = refs[0:group], refs[group:2 * group], refs[2 * group:3 * group]
    (sel_ref, later_ref, o_ref, od_ref, s0_scr, s1_scr, p0_scr, p1_scr, attn_acc_scr,
     qb_ref, m_ref, l_ref, acc_ref, suf_ref, snew_ref) = refs[3 * group:]
    step = (pl.program_id(0) * pl.num_programs(1) + pl.program_id(1)) * pl.num_programs(2) + pl.program_id(2)
    j = lax.rem(step, steps_per_seq)
    hd = FOX_HEAD_DIM

    @pl.when(j == 0)
    def _():
        q_row = q_ref[0]
        for c in range(D_MODEL // LANES):
            chunk = slice(c * LANES, (c + 1) * LANES)
            qb_ref[chunk, :] = jnp.broadcast_to(_row_to_col(q_row[:, chunk]), (LANES, PAGE_SIZE))
        s_new = jnp.sum(sel_ref[...] * (q_row * kn_ref[0]), axis=1, keepdims=True)
        snew_ref[...] = s_new
        m_ref[...] = s_new
        l_ref[...] = jnp.ones_like(l_ref)
        acc_ref[...] = jnp.zeros_like(acc_ref)
        suf_ref[...] = lfn_ref[0]

    def decode_pages():
        suf = suf_ref[...]
        bias = []
        for g in range(group):
            lf = clf_refs[g][0]
            bias.append(_dot_tri(lf, later_ref[...]) + suf)
            suf = suf + jnp.sum(lf, axis=1, keepdims=True)
        suf_ref[...] = suf
        bias = jnp.concatenate(bias, axis=1)
        for h in range(FOX_HEADS):
            rows = slice(h * hd, (h + 1) * hd)
            qb = qb_ref[rows, :]
            s = jnp.concatenate([jnp.sum(ck_refs[g][0, rows, :] * qb, axis=0, keepdims=True)
                                 for g in range(group)], axis=1) + bias[h:h + 1, :]
            m_old = m_ref[h:h + 1, :]
            m_new = jnp.maximum(m_old, jnp.max(s, axis=1, keepdims=True))
            corr = jnp.exp(m_old - m_new)
            p = jnp.exp(s - m_new)
            l_ref[h:h + 1, :] = corr * l_ref[h:h + 1, :] + jnp.sum(p, axis=1, keepdims=True)
            m_ref[h:h + 1, :] = m_new
            acc = corr * acc_ref[rows, :]
            for g in range(group):
                acc = acc + cv_refs[g][0, rows, :] * p[:, g * PAGE_SIZE:(g + 1) * PAGE_SIZE]
            acc_ref[rows, :] = acc

    _attention_tile(pl.program_id(2), qa_ref, ka_ref, v_ref, o_ref, s0_scr, s1_scr, p0_scr, p1_scr,
                    attn_acc_scr, tq, tk, decode_pages)

    @pl.when(j == steps_per_seq - 1)
    def _():
        sel = sel_ref[...]
        w_new = jnp.sum(sel * jnp.exp(snew_ref[...] - m_ref[...]), axis=0, keepdims=True)
        inv_l = jnp.sum(sel * (1.0 / l_ref[...]), axis=0, keepdims=True)
        eye = (lax.broadcasted_iota(jnp.int32, (LANES, LANES), 0)
               == lax.broadcasted_iota(jnp.int32, (LANES, LANES), 1))
        past = []
        for c in range(D_MODEL // LANES):
            col = jnp.sum(acc_ref[c * LANES:(c + 1) * LANES, :], axis=1, keepdims=True)
            past.append(jnp.sum(jnp.where(eye, col, 0.0), axis=0, keepdims=True))
        out = (jnp.concatenate(past, axis=1) + w_new * vn_ref[0]) * inv_l
        od_ref[0] = out.astype(BF16)


def _fox_mix(qa, ka, vtb, batch, seq, tq, tk, q, k_new, v_new, lf_new, cache_k, cache_v, cache_lf, page_table):
    nq = seq // tq
    assert (tq // tk) % 2 == 0, "key blocks alternate between two scratch slots"
    nb, n_pages = page_table.shape
    n_pool = cache_k.shape[0]
    n_steps = batch * FOX_PAIRS * nq
    group, rest = divmod(nb * n_pages, n_steps)
    assert rest == 0 and group > 0 and n_pages % group == 0, "cache pages must spread evenly over the grid steps"
    steps_per_seq = n_pages // group
    later = jnp.asarray(np.tril(np.ones((PAGE_SIZE, PAGE_SIZE), np.float32), -1), BF16)
    kt = jnp.transpose(cache_k, (0, 2, 3, 1)).reshape(n_pool, D_MODEL, PAGE_SIZE)
    vt = jnp.transpose(cache_v, (0, 2, 3, 1)).reshape(n_pool, D_MODEL, PAGE_SIZE)
    lft = jnp.transpose(cache_lf, (0, 2, 1))

    def step_of(b, p, i):
        return (b * FOX_PAIRS + p) * nq + i

    def page(g):
        def index(b, p, i, pt):
            s = step_of(b, p, i)
            return (pt[s // steps_per_seq, n_pages - 1 - ((s % steps_per_seq) * group + g)], 0, 0)
        return index

    def pages(rows):
        return [pl.BlockSpec((1, rows, PAGE_SIZE), page(g)) for g in range(group)]

    whole = lambda shape: pl.BlockSpec(shape, lambda b, p, i, pt: (0,) * len(shape))
    per_seq = lambda *shape: pl.BlockSpec((1,) + shape, lambda b, p, i, pt: (step_of(b, p, i) // steps_per_seq, 0, 0))
    grid_spec = pltpu.PrefetchScalarGridSpec(
        num_scalar_prefetch=1,
        grid=(batch, FOX_PAIRS, nq),
        in_specs=[pl.BlockSpec((1, 2 * LANES, tq), lambda b, p, i, pt: (b, p, i)),
                  pl.BlockSpec((seq, 2 * LANES), lambda b, p, i, pt: (b, p)),
                  pl.BlockSpec((1, LANES, seq), lambda b, p, i, pt: (b, p, 0)),
                  per_seq(1, D_MODEL), per_seq(1, D_MODEL), per_seq(1, D_MODEL), per_seq(FOX_HEADS, 1),
                  *pages(D_MODEL), *pages(D_MODEL), *pages(FOX_HEADS),
                  whole((FOX_HEADS, D_MODEL)), whole((PAGE_SIZE, PAGE_SIZE))],
        out_specs=[pl.BlockSpec((tq, LANES), lambda b, p, i, pt: (b * nq + i, p)), per_seq(1, D_MODEL)],
        scratch_shapes=[pltpu.VMEM((2, tk, tq), F32), pltpu.VMEM((2, tk, tq), F32),
                        pltpu.VMEM((2, tk, tq), BF16), pltpu.VMEM((2, tk, tq), BF16),
                        pltpu.VMEM((2, FOX_HEAD_DIM + ATTN_ONES_ROWS, tq), F32),
                        pltpu.VMEM((D_MODEL, PAGE_SIZE), F32),
                        pltpu.VMEM((FOX_HEADS, 1), F32), pltpu.VMEM((FOX_HEADS, 1), F32),
                        pltpu.VMEM((D_MODEL, PAGE_SIZE), F32),
                        pltpu.VMEM((FOX_HEADS, 1), F32), pltpu.VMEM((FOX_HEADS, 1), F32)],
    )
    row = lambda a: a.astype(F32).reshape(nb, 1, -1)
    attn, out = pl.pallas_call(
        functools.partial(_fox_mix_kernel, tq=tq, tk=tk, group=group, steps_per_seq=steps_per_seq),
        grid_spec=grid_spec,
        out_shape=[jax.ShapeDtypeStruct((batch * seq, D_MODEL), BF16),
                   jax.ShapeDtypeStruct((nb, 1, D_MODEL), BF16)],
        compiler_params=_params("arbitrary", "arbitrary", "arbitrary"),
        name="fox_mix",
    )(page_table, qa, ka, vtb, row(q), row(k_new), row(v_new), lf_new.reshape(nb, FOX_HEADS, 1),
      *([kt] * group), *([vt] * group), *([lft] * group), jnp.asarray(_head_rows()), later)
    return attn, out.reshape(nb, D_MODEL)


def _gla_proj_kernel(x_ref, wt_ref, wa2_ref, ba_ref, q_ref, k_ref, v_ref, r_ref, la_ref):
    xb = x_ref[...].astype(BF16)
    edges = (0, GLA_DK, 2 * GLA_DK, 2 * GLA_DK + GLA_DV, 2 * GLA_DK + 2 * GLA_DV, wt_ref.shape[0])
    q, k, v, r, a_lr = (_dot_nt(xb, wt_ref[lo:hi, :]) for lo, hi in zip(edges[:-1], edges[1:]))
    q_ref[...] = q
    k_ref[...] = k
    v_ref[...] = v.astype(BF16)
    r_ref[...] = r
    la_ref[...] = _log_sigmoid(_dot(a_lr.astype(BF16), wa2_ref[...]) + ba_ref[...]) / GLA_TAU


def _gla_proj(x, w_t, w_a2, b_a, tm):
    n = x.shape[0]
    row = pl.BlockSpec((tm, D_MODEL), lambda i: (i, 0))
    half = pl.BlockSpec((tm, GLA_DK), lambda i: (i, 0))
    return pl.pallas_call(
        _gla_proj_kernel,
        grid=(n // tm,),
        in_specs=[row, _resident(w_t.shape), _resident(w_a2.shape), _resident((1, GLA_DK))],
        out_specs=[half, half, row, row, half],
        out_shape=[jax.ShapeDtypeStruct((n, GLA_DK), F32), jax.ShapeDtypeStruct((n, GLA_DK), F32),
                   jax.ShapeDtypeStruct((n, GLA_DV), BF16), jax.ShapeDtypeStruct((n, GLA_DV), F32),
                   jax.ShapeDtypeStruct((n, GLA_DK), F32)],
        compiler_params=_params("parallel"),
        name="gla_proj",
    )(x, w_t, w_a2, b_a)


def _row_to_col(row):
    n = row.shape[1]
    eye = lax.broadcasted_iota(jnp.int32, (n, n), 0) == lax.broadcasted_iota(jnp.int32, (n, n), 1)
    return jnp.sum(jnp.where(eye, row, 0.0), axis=1, keepdims=True)


def _gla_chunk_kernel(q_ref, k_ref, v_ref, la_ref, tri_ref, o_ref, s_out_ref, s_ref, *, n_chunks):
    t = pl.program_id(2)

    @pl.when(t == 0)
    def _():
        s_ref[...] = jnp.zeros_like(s_ref)

    c = GLA_CHUNK
    ts = n_chunks * c
    chunk_rows = [slice(n * c, (n + 1) * c) for n in range(n_chunks)]
    bc = jnp.concatenate([_tri_dot(tri_ref[...], la_ref[r, :]) for r in chunk_rows], axis=0)
    b_last = [bc[r.stop - 1:r.stop, :] for r in chunk_rows]
    b_last_rows = jnp.concatenate([jnp.broadcast_to(b, (c, b.shape[1])) for b in b_last], axis=0)
    k = k_ref[...]
    v = v_ref[...]
    q_dec = (q_ref[...] * GLA_DK_H ** -0.5 * jnp.exp(bc)).astype(BF16)
    k_inv = (k * jnp.exp(-bc)).astype(BF16)
    k_end = (k * jnp.exp(b_last_rows - bc)).astype(BF16)
    row = lax.broadcasted_iota(jnp.int32, (ts, ts), 0)
    col = lax.broadcasted_iota(jnp.int32, (ts, ts), 1)
    same_chunk_causal = (col <= row) & (col >= (row & -c))
    attn = jnp.where(same_chunk_causal, _dot_nt(q_dec, k_inv), 0.0).astype(BF16)
    o_intra = _dot(attn, v)
    kv = [_dot_tn(k_end[r, :], v[r, :]) for r in chunk_rows]
    decay = [_row_to_col(jnp.exp(b)) for b in b_last]
    s = s_ref[...]
    s_before = []
    for n in range(n_chunks):
        s_before.append(s.astype(BF16))
        s = decay[n] * s + kv[n]
    s_ref[...] = s
    for n, r in enumerate(chunk_rows):
        o_ref[r, :] = o_intra[r, :] + _dot(q_dec[r, :], s_before[n])

    @pl.when(t == pl.num_programs(2) - 1)
    def _():
        s_out_ref[0, 0] = s


def _gla_chunked(q, k, v, la, batch, seq, ts):
    nt = seq // ts
    n_chunks = ts // GLA_CHUNK
    tri = jnp.asarray(np.tril(np.ones((GLA_CHUNK, GLA_CHUNK), np.float32)), BF16)
    key = pl.BlockSpec((ts, GLA_DK_H), lambda b, h, t: (b * nt + t, h))
    val = pl.BlockSpec((ts, GLA_DV_H), lambda b, h, t: (b * nt + t, h))
    return pl.pallas_call(
        functools.partial(_gla_chunk_kernel, n_chunks=n_chunks),
        grid=(batch, GLA_HEADS, nt),
        in_specs=[key, key, val, key, _resident((GLA_CHUNK, GLA_CHUNK))],
        out_specs=[val, pl.BlockSpec((1, 1, GLA_DK_H, GLA_DV_H), lambda b, h, t: (b, h, 0, 0))],
        out_shape=[jax.ShapeDtypeStruct((batch * seq, GLA_DV), F32),
                   jax.ShapeDtypeStruct((batch, GLA_HEADS, GLA_DK_H, GLA_DV_H), F32)],
        scratch_shapes=[pltpu.VMEM((GLA_DK_H, GLA_DV_H), F32)],
        compiler_params=_params("parallel", "parallel", "arbitrary"),
        name="gla_chunked",
    )(q, k, v, la, tri)


def _gla_step_kernel(q_ref, k_ref, v_ref, la_ref, s0_ref, o_ref, s_ref):
    for h in range(GLA_HEADS):
        la = la_ref[0, h]
        k = k_ref[0, h]
        v = v_ref[0, h]
        s0 = s0_ref[0, h]
        decay = jnp.exp(la)
        q_dec = (q_ref[0, h] * GLA_DK_H ** -0.5 * decay).astype(BF16)
        k_inv = (k * jnp.exp(-la)).astype(BF16)
        attn = jnp.sum(q_dec.astype(F32) * k_inv.astype(F32), axis=1, keepdims=True).astype(BF16).astype(F32)
        q_rows = jnp.broadcast_to(q_dec.astype(F32), (SUBLANES, GLA_DK_H)).astype(BF16)
        o_ref[0, h] = attn * v + _dot(q_rows, s0.astype(BF16))[0:1, :]
        s_ref[0, h] = _row_to_col(decay) * s0 + _row_to_col(k.astype(BF16).astype(F32)) * v


def _gla_step(q, k, v, la, s0):
    nb = q.shape[0]
    heads = lambda a, width: a.astype(F32).reshape(nb, GLA_HEADS, 1, width)
    key = pl.BlockSpec((1, GLA_HEADS, 1, GLA_DK_H), lambda b: (b, 0, 0, 0))
    val = pl.BlockSpec((1, GLA_HEADS, 1, GLA_DV_H), lambda b: (b, 0, 0, 0))
    state = pl.BlockSpec((1, GLA_HEADS, GLA_DK_H, GLA_DV_H), lambda b: (b, 0, 0, 0))
    o, s = pl.pallas_call(
        _gla_step_kernel,
        grid=(nb,),
        in_specs=[key, key, val, key, state],
        out_specs=[val, state],
        out_shape=[jax.ShapeDtypeStruct((nb, GLA_HEADS, 1, GLA_DV_H), F32),
                   jax.ShapeDtypeStruct((nb, GLA_HEADS, GLA_DK_H, GLA_DV_H), F32)],
        compiler_params=_params("parallel"),
        name="gla_step",
    )(heads(q, GLA_DK_H), heads(k, GLA_DK_H), heads(v, GLA_DV_H), heads(la, GLA_DK_H), s0)
    return o.reshape(nb, GLA_DV), s


PROMPT_TILE = 512
WIDE_TILE = 1024
ATTN_Q_TILE = 512
ATTN_K_TILE = 256
GLA_TILE = 512


def kernel(x_prompt, x_sample, cache_fox_k, cache_fox_v, cache_fox_logf, state_gla, page_table,
           ln_g, ln_b, ffn_w_in, ffn_w_out, fox_w_in, fox_b_f, fox_w_o,
           gla_w_in, gla_w_a2, gla_b_a, gla_norm_g, gla_w_o):
    batch, seq, _ = x_prompt.shape
    nb = x_sample.shape[0]
    xp = x_prompt.reshape(batch * seq, D_MODEL)
    xs = x_sample.reshape(nb, D_MODEL)
    tp = min(PROMPT_TILE, batch * seq)
    tw = min(WIDE_TILE, batch * seq)
    vec = lambda a: a.reshape(1, -1)
    w_in = ffn_w_in.astype(BF16)
    w_out = ffn_w_out.astype(BF16)

    def ffn(i, half, ln):
        return ((w_in, (i, half)), (w_out, (i, half)), vec(ln_g[i, ln]), vec(ln_b[i, ln]))

    xp, xs = _ffn_half(xp, xs, *ffn(0, 0, 0), tw)
    w_fox_t = fox_w_in[0].T.astype(BF16)
    kt, vt, vtb, lft, qa, ka = _fox_proj_t(xp, w_fox_t, vec(fox_b_f[0]), batch, seq, min(PROMPT_TILE, seq))
    qs, ks, vs, lfs = _fox_proj(xs, w_fox_t, vec(fox_b_f[0]), nb)
    mp, ms = _fox_mix(qa, ka, vtb, batch, seq, min(ATTN_Q_TILE, seq), min(ATTN_K_TILE, seq),
                      qs, ks, vs, lfs, cache_fox_k[0], cache_fox_v[0], cache_fox_logf[0], page_table)
    mix = (fox_w_o[0].astype(BF16), vec(ln_g[0, 1]), vec(ln_b[0, 1])) + ffn(0, 1, 2)
    xp, xs = _mix_ffn(xp, mp, xs, ms, *mix, tp)

    xp, xs = _ffn_half(xp, xs, *ffn(1, 0, 0), tw)
    proj = (gla_w_in[0].T.astype(BF16), gla_w_a2[0].astype(BF16), vec(gla_b_a[0]))
    gqp, gkp, gvp, grp, glap = _gla_proj(xp, *proj, tw)
    gqs, gks, gvs, grs, glas = _gla_proj(xs, *proj, nb)
    op, state_p = _gla_chunked(gqp, gkp, gvp, glap, batch, seq, min(GLA_TILE, seq))
    os_, state_s = _gla_step(gqs, gks, gvs, glas, state_gla[0])
    mix = (vec(gla_norm_g[0]), gla_w_o[0].astype(BF16), vec(ln_g[1, 1]), vec(ln_b[1, 1])) + ffn(1, 1, 2)
    xp, xs = _gla_mix_ffn(xp, op, grp, xs, os_, grs, *mix, tp)

    heads = (FOX_HEADS, FOX_HEAD_DIM)
    token_major = lambda a: jnp.transpose(a.reshape(batch, *heads, seq), (0, 3, 1, 2))[None]
    return (xp.reshape(batch, seq, D_MODEL), xs.reshape(nb, 1, D_MODEL),
            token_major(kt), token_major(vt), jnp.transpose(lft, (0, 2, 1))[None],
            ks.reshape(1, nb, 1, *heads), vs.reshape(1, nb, 1, *heads), lfs.reshape(1, nb, 1, FOX_HEADS),
            state_p[None], state_s[None])
```

```python
import functools

import numpy as np
import jax
import jax.numpy as jnp
from jax import lax
from jax.experimental import pallas as pl
from jax.experimental.pallas import tpu as pltpu

F32 = jnp.float32
BF16 = jnp.bfloat16

D_MODEL = 1024
DEPTH = 2
PAGE_SIZE = 128
FOX_HEADS = 16
FOX_HEAD_DIM = D_MODEL // FOX_HEADS
FOX_PAIRS = FOX_HEADS // 2
GLA_HEADS = 4
GLA_DK = D_MODEL // 2
GLA_DV = D_MODEL
GLA_DK_H = GLA_DK // GLA_HEADS
GLA_DV_H = GLA_DV // GLA_HEADS
GLA_TAU = 16.0
GLA_CHUNK = 64
D_FF = ((8 * D_MODEL // 3 + 127) // 128) * 128
DN_ALPHA = (2 * DEPTH) ** 0.25
LN_EPS = 1e-5
MASK_VALUE = -1e30
LOG2E = 1.4426950408889634

LANES = 128
SUBLANES = 8
MXU_WIDTH = 256
FF_CHUNK = MXU_WIDTH
AUG_PARTS = 3
VMEM_LIMIT = 56 * 2 ** 20
ATTN_ONES_ROWS = 16
ATTN_UNROLL = 4


def _params(*sem):
    return pltpu.CompilerParams(dimension_semantics=sem, vmem_limit_bytes=VMEM_LIMIT)


def _resident(shape):
    return pl.BlockSpec(shape, lambda *_: (0,) * len(shape), pipeline_mode=pl.Buffered(1))


def _stacked_spec(stacked):
    w, index = stacked
    tail = w.shape[len(index):]
    return pl.BlockSpec((None,) * len(index) + tail, lambda *_: index + (0,) * len(tail),
                        pipeline_mode=pl.Buffered(1))


def _dot(a, b):
    return jnp.dot(a, b, preferred_element_type=F32)


def _dot_nt(a, b):
    return lax.dot_general(a, b, (((1,), (1,)), ((), ())), preferred_element_type=F32)


def _dot_tn(a, b):
    return lax.dot_general(a, b, (((0,), (0,)), ((), ())), preferred_element_type=F32)


def _layer_norm(y, g, b):
    mu = jnp.mean(y, axis=-1, keepdims=True)
    d = y - mu
    var = jnp.mean(d * d, axis=-1, keepdims=True)
    return d * lax.rsqrt(var + LN_EPS) * g + b


def _log_sigmoid(z):
    return jnp.minimum(z, 0.0) - jnp.log1p(jnp.exp(-jnp.abs(z)))


def _split3(x):
    hi = x.astype(BF16).astype(F32)
    r = x - hi
    mid = r.astype(BF16).astype(F32)
    lo = (r - mid).astype(BF16).astype(F32)
    return hi, mid, lo


def _tri_dot(tri, x):
    hi, mid, lo = _split3(x)
    return (_dot(tri, hi.astype(BF16)) + _dot(tri, mid.astype(BF16))) + _dot(tri, lo.astype(BF16))


def _dot_tri(x, tri):
    hi, mid, lo = _split3(x)
    return (_dot(hi.astype(BF16), tri) + _dot(mid.astype(BF16), tri)) + _dot(lo.astype(BF16), tri)


def _swiglu_ln(x, win_ref, wout_ref, g, b):
    xb = x.astype(BF16)
    acc = jnp.zeros(x.shape, F32)
    for c in range(D_FF // FF_CHUNK):
        lo = c * FF_CHUNK
        gate = _dot(xb, win_ref[:, lo:lo + FF_CHUNK])
        up = _dot(xb, win_ref[:, D_FF + lo:D_FF + lo + FF_CHUNK])
        h = (gate * jax.nn.sigmoid(gate) * up).astype(BF16)
        acc = acc + _dot(h, wout_ref[lo:lo + FF_CHUNK, :])
    return _layer_norm(DN_ALPHA * x + 0.5 * acc, g, b)


def _on_last_step(fn):
    pl.when(pl.program_id(0) == pl.num_programs(0) - 1)(fn)


def _sample_rows(a):
    return pl.BlockSpec(a.shape, lambda i: (0,) * len(a.shape))


def _ffn_kernel(x_ref, xs_ref, win_ref, wout_ref, g_ref, b_ref, o_ref, os_ref):
    def rows(x_ref, o_ref):
        o_ref[...] = _swiglu_ln(x_ref[...], win_ref, wout_ref, g_ref[...], b_ref[...])

    rows(x_ref, o_ref)
    _on_last_step(lambda: rows(xs_ref, os_ref))


def _ffn_half(x, xs, w_in, w_out, g, b, tm):
    n = x.shape[0]
    row = pl.BlockSpec((tm, D_MODEL), lambda i: (i, 0))
    return pl.pallas_call(
        _ffn_kernel,
        grid=(n // tm,),
        in_specs=[row, _sample_rows(xs), _stacked_spec(w_in), _stacked_spec(w_out),
                  _resident((1, D_MODEL)), _resident((1, D_MODEL))],
        out_specs=[row, _sample_rows(xs)],
        out_shape=[jax.ShapeDtypeStruct((n, D_MODEL), F32), jax.ShapeDtypeStruct(xs.shape, F32)],
        compiler_params=_params("arbitrary"),
        name="ffn_half",
    )(x, xs, w_in[0], w_out[0], g, b)


def _mix_ffn_kernel(x_ref, m_ref, xs_ref, ms_ref, wo_ref, g1_ref, b1_ref, win_ref, wout_ref, g2_ref, b2_ref,
                    o_ref, os_ref):
    def rows(x_ref, m_ref, o_ref):
        x1 = _layer_norm(DN_ALPHA * x_ref[...] + _dot(m_ref[...], wo_ref[...]), g1_ref[...], b1_ref[...])
        o_ref[...] = _swiglu_ln(x1, win_ref, wout_ref, g2_ref[...], b2_ref[...])

    rows(x_ref, m_ref, o_ref)
    _on_last_step(lambda: rows(xs_ref, ms_ref, os_ref))


def _gla_mix_ffn_kernel(x_ref, att_ref, r_ref, xs_ref, atts_ref, rs_ref, gn_ref, wo_ref, g1_ref, b1_ref,
                        win_ref, wout_ref, g2_ref, b2_ref, o_ref, os_ref):
    refs = (gn_ref, wo_ref, g1_ref, b1_ref, win_ref, wout_ref, g2_ref, b2_ref)
    _gla_mix_rows(x_ref, att_ref, r_ref, *refs, o_ref)
    _on_last_step(lambda: _gla_mix_rows(xs_ref, atts_ref, rs_ref, *refs, os_ref))


def _gla_mix_rows(x_ref, att_ref, r_ref, gn_ref, wo_ref, g1_ref, b1_ref, win_ref, wout_ref, g2_ref, b2_ref, o_ref):
    heads = []
    for h in range(GLA_HEADS):
        o = att_ref[:, h * GLA_DV_H:(h + 1) * GLA_DV_H]
        mu = jnp.mean(o, axis=-1, keepdims=True)
        d = o - mu
        var = jnp.mean(d * d, axis=-1, keepdims=True)
        heads.append(d * lax.rsqrt(var + LN_EPS))
    on = jnp.concatenate(heads, axis=-1) * gn_ref[...]
    r = r_ref[...]
    m = (on * (r * jax.nn.sigmoid(r))).astype(BF16)
    x1 = _layer_norm(DN_ALPHA * x_ref[...] + _dot(m, wo_ref[...]), g1_ref[...], b1_ref[...])
    o_ref[...] = _swiglu_ln(x1, win_ref, wout_ref, g2_ref[...], b2_ref[...])


def _mix_ffn(x, m, xs, ms, w_o, g1, b1, w_in, w_out, g2, b2, tm):
    n = x.shape[0]
    row = pl.BlockSpec((tm, D_MODEL), lambda i: (i, 0))
    vec = _resident((1, D_MODEL))
    return pl.pallas_call(
        _mix_ffn_kernel,
        grid=(n // tm,),
        in_specs=[row, row, _sample_rows(xs), _sample_rows(ms), _resident(w_o.shape), vec, vec,
                  _stacked_spec(w_in), _stacked_spec(w_out), vec, vec],
        out_specs=[row, _sample_rows(xs)],
        out_shape=[jax.ShapeDtypeStruct((n, D_MODEL), F32), jax.ShapeDtypeStruct(xs.shape, F32)],
        compiler_params=_params("arbitrary"),
        name="mix_ffn",
    )(x, m, xs, ms, w_o, g1, b1, w_in[0], w_out[0], g2, b2)


def _gla_mix_ffn(x, o, r, xs, os_, rs, gn, w_o, g1, b1, w_in, w_out, g2, b2, tm):
    n = x.shape[0]
    row = pl.BlockSpec((tm, D_MODEL), lambda i: (i, 0))
    vec = _resident((1, D_MODEL))
    return pl.pallas_call(
        _gla_mix_ffn_kernel,
        grid=(n // tm,),
        in_specs=[row, row, row, _sample_rows(xs), _sample_rows(os_), _sample_rows(rs),
                  vec, _resident(w_o.shape), vec, vec,
                  _stacked_spec(w_in), _stacked_spec(w_out), vec, vec],
        out_specs=[row, _sample_rows(xs)],
        out_shape=[jax.ShapeDtypeStruct((n, D_MODEL), F32), jax.ShapeDtypeStruct(xs.shape, F32)],
        compiler_params=_params("arbitrary"),
        name="gla_mix_ffn",
    )(x, o, r, xs, os_, rs, gn, w_o, g1, b1, w_in[0], w_out[0], g2, b2)


def _fox_proj_kernel(x_ref, wt_ref, bf_ref, q_ref, k_ref, v_ref, lf_ref):
    xb = x_ref[...].astype(BF16)
    q = _dot_nt(xb, wt_ref[0:D_MODEL, :])
    q_ref[...] = (q * FOX_HEAD_DIM ** -0.5).astype(BF16)
    k_ref[...] = _dot_nt(xb, wt_ref[D_MODEL:2 * D_MODEL, :])
    v_ref[...] = _dot_nt(xb, wt_ref[2 * D_MODEL:3 * D_MODEL, :])
    lf_ref[...] = _log_sigmoid(_dot_nt(xb, wt_ref[3 * D_MODEL:3 * D_MODEL + FOX_HEADS, :]) + bf_ref[...])


def _fox_proj(x, w_t, b_f, tm):
    n = x.shape[0]
    row = pl.BlockSpec((tm, D_MODEL), lambda i: (i, 0))
    gate = pl.BlockSpec((tm, FOX_HEADS), lambda i: (i, 0))
    full = lambda dt: jax.ShapeDtypeStruct((n, D_MODEL), dt)
    return pl.pallas_call(
        _fox_proj_kernel,
        grid=(n // tm,),
        in_specs=[row, _resident(w_t.shape), _resident((1, FOX_HEADS))],
        out_specs=[row, row, row, gate],
        out_shape=[full(BF16), full(F32), full(F32), jax.ShapeDtypeStruct((n, FOX_HEADS), F32)],
        compiler_params=_params("parallel"),
        name="fox_proj",
    )(x, w_t, b_f)


def _fox_proj_t_kernel(x_ref, wt_ref, bf_ref, bft_ref, tril_ref, triu_ref, pk_ref, ok_ref,
                       kt_ref, vt_ref, vtb_ref, lft_ref, qa_ref, ka_ref, carry_row_ref, carry_col_ref):
    xb = x_ref[...].astype(BF16)
    q_t = (_dot_nt(wt_ref[0:D_MODEL, :], xb) * (FOX_HEAD_DIM ** -0.5 * LOG2E)).astype(BF16)
    kt_ref[0] = _dot_nt(wt_ref[D_MODEL:2 * D_MODEL, :], xb)
    k = _dot_nt(xb, wt_ref[D_MODEL:2 * D_MODEL, :])
    v_t = _dot_nt(wt_ref[2 * D_MODEL:3 * D_MODEL, :], xb)
    vt_ref[0] = v_t
    vtb_ref[0] = v_t.astype(BF16)
    w_gate = wt_ref[3 * D_MODEL:3 * D_MODEL + FOX_HEADS, :]
    lf = _log_sigmoid(_dot_nt(xb, w_gate) + bf_ref[...])
    lf_t = _log_sigmoid(_dot_nt(w_gate, xb) + bft_ref[...])
    lft_ref[0] = lf_t
    _write_augmented(q_t, k, lf, lf_t, tril_ref, triu_ref, pk_ref, ok_ref, qa_ref, ka_ref,
                     carry_row_ref, carry_col_ref)


def _fox_proj_t(x, w_t, b_f, batch, seq, tm):
    pk, ones_k = _aug_placement()
    tril = jnp.asarray(np.tril(np.ones((tm, tm), np.float32)), BF16)
    triu = jnp.asarray(np.triu(np.ones((tm, tm), np.float32)), BF16)
    nt = seq // tm
    row = pl.BlockSpec((tm, D_MODEL), lambda b, t: (b * nt + t, 0))
    wide = pl.BlockSpec((tm, 2 * D_MODEL), lambda b, t: (b * nt + t, 0))
    col = pl.BlockSpec((1, D_MODEL, tm), lambda b, t: (b, 0, t))
    tall = pl.BlockSpec((1, 2 * D_MODEL, tm), lambda b, t: (b, 0, t))
    gate_t = pl.BlockSpec((1, FOX_HEADS, tm), lambda b, t: (b, 0, t))
    feat = lambda dt: jax.ShapeDtypeStruct((batch, D_MODEL, seq), dt)
    return pl.pallas_call(
        _fox_proj_t_kernel,
        grid=(batch, nt),
        in_specs=[row, _resident(w_t.shape), _resident((1, FOX_HEADS)), _resident((FOX_HEADS, 1)),
                  _resident((tm, tm)), _resident((tm, tm)), _resident(pk.shape), _resident((1, D_MODEL))],
        out_specs=[col, col, col, gate_t, tall, wide],
        out_shape=[feat(F32), feat(F32), feat(BF16), jax.ShapeDtypeStruct((batch, FOX_HEADS, seq), F32),
                   jax.ShapeDtypeStruct((batch, 2 * D_MODEL, seq), BF16),
                   jax.ShapeDtypeStruct((batch * seq, 2 * D_MODEL), BF16)],
        scratch_shapes=[pltpu.VMEM((1, FOX_HEADS), F32), pltpu.VMEM((FOX_HEADS, 1), F32)],
        compiler_params=_params("parallel", "arbitrary"),
        name="fox_proj_t",
    )(x, w_t, b_f, b_f.reshape(FOX_HEADS, 1), tril, triu, jnp.asarray(pk, BF16), jnp.asarray(ones_k))


def _aug_placement():
    pk = np.zeros((AUG_PARTS, FOX_HEADS, D_MODEL), np.float32)
    ones_k = np.zeros((1, D_MODEL), np.float32)
    for h in range(FOX_HEADS):
        base = (h // 2) * LANES + (FOX_HEAD_DIM if h % 2 == 0 else 0)
        for j in range(AUG_PARTS):
            pk[j, h, base + j] = -1.0
            ones_k[0, base + AUG_PARTS + j] = 1.0
    return pk, ones_k


def _write_augmented(q_t, k, lf, lf_t, tril_ref, triu_ref, pk_ref, ok_ref,
                     qa_ref, ka_ref, carry_row_ref, carry_col_ref):
    @pl.when(pl.program_id(1) == 0)
    def _():
        carry_row_ref[...] = jnp.zeros_like(carry_row_ref)
        carry_col_ref[...] = jnp.zeros_like(carry_col_ref)

    tm = k.shape[0]
    c = _tri_dot(tril_ref[...], lf) + carry_row_ref[...]
    c_t = _dot_tri(lf_t, triu_ref[...]) + carry_col_ref[...]
    carry_row_ref[...] = c[tm - 1:tm, :]
    carry_col_ref[...] = c_t[:, tm - 1:tm]

    aug_k = ok_ref[...]
    for j, part in enumerate(_split3(c * LOG2E)):
        aug_k = aug_k + _dot(part.astype(BF16), pk_ref[j])
    lane = lax.broadcasted_iota(jnp.int32, (tm, D_MODEL), 1) & (LANES - 1)
    low = lane < FOX_HEAD_DIM
    k_even = jnp.where(low, k, aug_k).astype(BF16)
    k_odd = jnp.where(low, aug_k, k).astype(BF16)
    for p in range(FOX_PAIRS):
        src = slice(p * LANES, (p + 1) * LANES)
        ka_ref[:, 2 * p * LANES:(2 * p + 1) * LANES] = k_even[:, src]
        ka_ref[:, (2 * p + 1) * LANES:(2 * p + 2) * LANES] = k_odd[:, src]

    parts = _split3(c_t * LOG2E)
    group = 2 * SUBLANES
    r = lax.broadcasted_iota(jnp.int32, (group, tm), 0)
    padding = jnp.zeros((FOX_HEAD_DIM - group, tm), BF16)
    for h in range(FOX_HEADS):
        spare = jnp.where(r < AUG_PARTS, 1.0, 0.0)
        for j, part in enumerate(parts):
            spare = jnp.where(r == AUG_PARTS + j, part[h:h + 1, :], spare)
        feat = slice(h * FOX_HEAD_DIM, (h + 1) * FOX_HEAD_DIM)
        first = slice(h * LANES, h * LANES + FOX_HEAD_DIM)
        second = slice(h * LANES + FOX_HEAD_DIM, (h + 1) * LANES)
        q_rows, spare_rows = (first, second) if h % 2 == 0 else (second, first)
        qa_ref[0, q_rows, :] = q_t[feat, :]
        qa_ref[0, spare_rows, :] = jnp.concatenate([spare.astype(BF16), padding], axis=0)


def _attention_tile(i, qa_ref, ka_ref, v_ref, o_ref, s0_scr, s1_scr, p0_scr, p1_scr, acc_scr, tq, tk, alongside):
    q_t = (qa_ref[0, 0:LANES, :], qa_ref[0, LANES:2 * LANES, :])
    hd = FOX_HEAD_DIM
    heads = range(2)
    s_scr = (s0_scr, s1_scr)
    p_scr = (p0_scr, p1_scr)
    ones_rows = jnp.ones((ATTN_ONES_ROWS, tk), BF16)

    def logits(b, slot, lo):
        start = pl.multiple_of(b * tk, tk)
        for e in heads:
            s_scr[slot][e, :, lo:] = _dot(ka_ref[pl.ds(start, tk), e * LANES:(e + 1) * LANES], q_t[e][:, lo:])

    def softmax(e, slot, m_old, mask, lo):
        s = s_scr[slot][e, :, lo:]
        if mask is not None:
            s = jnp.where(mask[:, lo:], s, MASK_VALUE)
        m_prev = m_old[e][:, lo:]
        m_new = jnp.maximum(m_prev, jnp.max(s, axis=0, keepdims=True))
        p_scr[slot][e, :, lo:] = jnp.exp2(s - m_new).astype(BF16)
        corr = jnp.exp2(m_prev - m_new)
        if lo:
            m_new = jnp.concatenate([m_old[e][:, :lo], m_new], axis=1)
            corr = jnp.concatenate([jnp.ones((1, lo), F32), corr], axis=1)
        return m_new, corr

    def weighted_values(b, slot, corr, lo):
        start = pl.multiple_of(b * tk, tk)
        for e in heads:
            v_ext = jnp.concatenate([v_ref[0, e * hd:(e + 1) * hd, pl.ds(start, tk)], ones_rows], axis=0)
            acc_scr[e, :, lo:] = corr[e][:, lo:] * acc_scr[e, :, lo:] + _dot(v_ext, p_scr[slot][e, :, lo:])

    def step(b, slot, m_old, corr_prev, mask=None, lo=0, last=False):
        weighted_values(jnp.maximum(b - 1, 0), 1 - slot, corr_prev, max(lo - tk, 0))
        first = softmax(0, slot, m_old, mask, lo)
        if not last:
            logits(b + 1, 1 - slot, lo + tk if mask is not None else 0)
        second = softmax(1, slot, m_old, mask, lo)
        return tuple(zip(first, second))

    per_q = tq // tk

    def full_blocks(first, count):
        def body(t, carry):
            for d in range(count):
                carry = step(first + count * t + d, d % 2, *carry)
            return carry
        return body

    logits(0, 0, 0)
    p1_scr[...] = jnp.zeros_like(p1_scr)
    acc_scr[...] = jnp.zeros_like(acc_scr)
    carry = ((jnp.full((1, tq), MASK_VALUE, F32),) * 2, (jnp.ones((1, tq), F32),) * 2)
    n_full = i * per_q
    long_trip = ATTN_UNROLL * per_q
    n_long = n_full // long_trip
    carry = lax.fori_loop(0, n_long, full_blocks(0, long_trip), carry)
    carry = lax.fori_loop(0, (n_full - n_long * long_trip) // per_q, full_blocks(n_long * long_trip, per_q), carry)
    key = lax.broadcasted_iota(jnp.int32, (tk, tq), 0)
    query = lax.broadcasted_iota(jnp.int32, (tk, tq), 1)
    for d in range(per_q):
        carry = step(n_full + d, d % 2, *carry, mask=key + d * tk <= query, lo=d * tk, last=d == per_q - 1)
    alongside()
    weighted_values(n_full + per_q - 1, (per_q - 1) % 2, carry[1], (per_q - 1) * tk)
    out_t = jnp.concatenate([acc_scr[e, 0:hd, :] * (1.0 / acc_scr[e, hd:hd + 1, :]) for e in heads], axis=0)
    o_ref[...] = out_t.T.astype(BF16)


def _head_rows():
    sel = np.zeros((FOX_HEADS, D_MODEL), np.float32)
    for h in range(FOX_HEADS):
        sel[h, h * FOX_HEAD_DIM:(h + 1) * FOX_HEAD_DIM] = 1.0
    return sel


def _fox_mix_kernel(pt_ref, qa_ref, ka_ref, v_ref, q_ref, kn_ref, vn_ref, lfn_ref, *refs,
                    tq, tk, group, steps_per_seq):
    ck_refs, cv_refs, clf_refs = refs[0:group], refs[group:2 * group], refs[2 * group:3 * group]
    (sel_ref, later_ref, o_ref, od_ref, s0_scr, s1_scr, p0_scr, p1_scr, attn_acc_scr,
     qb_ref, m_ref, l_ref, acc_ref, suf_ref, snew_ref) = refs[3 * group:]
    step = (pl.program_id(0) * pl.num_programs(1) + pl.program_id(1)) * pl.num_programs(2) + pl.program_id(2)
    j = lax.rem(step, steps_per_seq)
    hd = FOX_HEAD_DIM

    @pl.when(j == 0)
    def _():
        q_row = q_ref[0]
        for c in range(D_MODEL // LANES):
            chunk = slice(c * LANES, (c + 1) * LANES)
            qb_ref[chunk, :] = jnp.broadcast_to(_row_to_col(q_row[:, chunk]), (LANES, PAGE_SIZE))
        s_new = jnp.sum(sel_ref[...] * (q_row * kn_ref[0]), axis=1, keepdims=True)
        snew_ref[...] = s_new
        m_ref[...] = s_new
        l_ref[...] = jnp.ones_like(l_ref)
        acc_ref[...] = jnp.zeros_like(acc_ref)
        suf_ref[...] = lfn_ref[0]

    def decode_pages():
        suf = suf_ref[...]
        bias = []
        for g in range(group):
            lf = clf_refs[g][0]
            bias.append(_dot_tri(lf, later_ref[...]) + suf)
            suf = suf + jnp.sum(lf, axis=1, keepdims=True)
        suf_ref[...] = suf
        bias = jnp.concatenate(bias, axis=1)
        for h in range(FOX_HEADS):
            rows = slice(h * hd, (h + 1) * hd)
            qb = qb_ref[rows, :]
            s = jnp.concatenate([jnp.sum(ck_refs[g][0, rows, :] * qb, axis=0, keepdims=True)
                                 for g in range(group)], axis=1) + bias[h:h + 1, :]
            m_old = m_ref[h:h + 1, :]
            m_new = jnp.maximum(m_old, jnp.max(s, axis=1, keepdims=True))
            corr = jnp.exp(m_old - m_new)
            p = jnp.exp(s - m_new)
            l_ref[h:h + 1, :] = corr * l_ref[h:h + 1, :] + jnp.sum(p, axis=1, keepdims=True)
            m_ref[h:h + 1, :] = m_new
            acc = corr * acc_ref[rows, :]
            for g in range(group):
                acc = acc + cv_refs[g][0, rows, :] * p[:, g * PAGE_SIZE:(g + 1) * PAGE_SIZE]
            acc_ref[rows, :] = acc

    _attention_tile(pl.program_id(2), qa_ref, ka_ref, v_ref, o_ref, s0_scr, s1_scr, p0_scr, p1_scr,
                    attn_acc_scr, tq, tk, decode_pages)

    @pl.when(j == steps_per_seq - 1)
    def _():
        sel = sel_ref[...]
        w_new = jnp.sum(sel * jnp.exp(snew_ref[...] - m_ref[...]), axis=0, keepdims=True)
        inv_l = jnp.sum(sel * (1.0 / l_ref[...]), axis=0, keepdims=True)
        eye = (lax.broadcasted_iota(jnp.int32, (LANES, LANES), 0)
               == lax.broadcasted_iota(jnp.int32, (LANES, LANES), 1))
        past = []
        for c in range(D_MODEL // LANES):
            col = jnp.sum(acc_ref[c * LANES:(c + 1) * LANES, :], axis=1, keepdims=True)
            past.append(jnp.sum(jnp.where(eye, col, 0.0), axis=0, keepdims=True))
        out = (jnp.concatenate(past, axis=1) + w_new * vn_ref[0]) * inv_l
        od_ref[0] = out.astype(BF16)


def _fox_mix(qa, ka, vtb, batch, seq, tq, tk, q, k_new, v_new, lf_new, cache_k, cache_v, cache_lf, page_table):
    nq = seq // tq
    assert (tq // tk) % 2 == 0, "key blocks alternate between two scratch slots"
    nb, n_pages = page_table.shape
    n_pool = cache_k.shape[0]
    n_steps = batch * FOX_PAIRS * nq
    group, rest = divmod(nb * n_pages, n_steps)
    assert rest == 0 and group > 0 and n_pages % group == 0, "cache pages must spread evenly over the grid steps"
    steps_per_seq = n_pages // group
    later = jnp.asarray(np.tril(np.ones((PAGE_SIZE, PAGE_SIZE), np.float32), -1), BF16)
    kt = jnp.transpose(cache_k, (0, 2, 3, 1)).reshape(n_pool, D_MODEL, PAGE_SIZE)
    vt = jnp.transpose(cache_v, (0, 2, 3, 1)).reshape(n_pool, D_MODEL, PAGE_SIZE)
    lft = jnp.transpose(cache_lf, (0, 2, 1))

    def step_of(b, p, i):
        return (b * FOX_PAIRS + p) * nq + i

    def page(g):
        def index(b, p, i, pt):
            s = step_of(b, p, i)
            return (pt[s // steps_per_seq, n_pages - 1 - ((s % steps_per_seq) * group + g)], 0, 0)
        return index

    def pages(rows):
        return [pl.BlockSpec((1, rows, PAGE_SIZE), page(g)) for g in range(group)]

    whole = lambda shape: pl.BlockSpec(shape, lambda b, p, i, pt: (0,) * len(shape))
    per_seq = lambda *shape: pl.BlockSpec((1,) + shape, lambda b, p, i, pt: (step_of(b, p, i) // steps_per_seq, 0, 0))
    grid_spec = pltpu.PrefetchScalarGridSpec(
        num_scalar_prefetch=1,
        grid=(batch, FOX_PAIRS, nq),
        in_specs=[pl.BlockSpec((1, 2 * LANES, tq), lambda b, p, i, pt: (b, p, i)),
                  pl.BlockSpec((seq, 2 * LANES), lambda b, p, i, pt: (b, p)),
                  pl.BlockSpec((1, LANES, seq), lambda b, p, i, pt: (b, p, 0)),
                  per_seq(1, D_MODEL), per_seq(1, D_MODEL), per_seq(1, D_MODEL), per_seq(FOX_HEADS, 1),
                  *pages(D_MODEL), *pages(D_MODEL), *pages(FOX_HEADS),
                  whole((FOX_HEADS, D_MODEL)), whole((PAGE_SIZE, PAGE_SIZE))],
        out_specs=[pl.BlockSpec((tq, LANES), lambda b, p, i, pt: (b * nq + i, p)), per_seq(1, D_MODEL)],
        scratch_shapes=[pltpu.VMEM((2, tk, tq), F32), pltpu.VMEM((2, tk, tq), F32),
                        pltpu.VMEM((2, tk, tq), BF16), pltpu.VMEM((2, tk, tq), BF16),
                        pltpu.VMEM((2, FOX_HEAD_DIM + ATTN_ONES_ROWS, tq), F32),
                        pltpu.VMEM((D_MODEL, PAGE_SIZE), F32),
                        pltpu.VMEM((FOX_HEADS, 1), F32), pltpu.VMEM((FOX_HEADS, 1), F32),
                        pltpu.VMEM((D_MODEL, PAGE_SIZE), F32),
                        pltpu.VMEM((FOX_HEADS, 1), F32), pltpu.VMEM((FOX_HEADS, 1), F32)],
    )
    row = lambda a: a.astype(F32).reshape(nb, 1, -1)
    attn, out = pl.pallas_call(
        functools.partial(_fox_mix_kernel, tq=tq, tk=tk, group=group, steps_per_seq=steps_per_seq),
        grid_spec=grid_spec,
        out_shape=[jax.ShapeDtypeStruct((batch * seq, D_MODEL), BF16),
                   jax.ShapeDtypeStruct((nb, 1, D_MODEL), BF16)],
        compiler_params=_params("arbitrary", "arbitrary", "arbitrary"),
        name="fox_mix",
    )(page_table, qa, ka, vtb, row(q), row(k_new), row(v_new), lf_new.reshape(nb, FOX_HEADS, 1),
      *([kt] * group), *([vt] * group), *([lft] * group), jnp.asarray(_head_rows()), later)
    return attn, out.reshape(nb, D_MODEL)


def _gla_proj_kernel(x_ref, wt_ref, wa2_ref, ba_ref, q_ref, k_ref, v_ref, r_ref, la_ref):
    xb = x_ref[...].astype(BF16)
    edges = (0, GLA_DK, 2 * GLA_DK, 2 * GLA_DK + GLA_DV, 2 * GLA_DK + 2 * GLA_DV, wt_ref.shape[0])
    q, k, v, r, a_lr = (_dot_nt(xb, wt_ref[lo:hi, :]) for lo, hi in zip(edges[:-1], edges[1:]))
    q_ref[...] = q
    k_ref[...] = k
    v_ref[...] = v.astype(BF16)
    r_ref[...] = r
    la_ref[...] = _log_sigmoid(_dot(a_lr.astype(BF16), wa2_ref[...]) + ba_ref[...]) / GLA_TAU


def _gla_proj(x, w_t, w_a2, b_a, tm):
    n = x.shape[0]
    row = pl.BlockSpec((tm, D_MODEL), lambda i: (i, 0))
    half = pl.BlockSpec((tm, GLA_DK), lambda i: (i, 0))
    return pl.pallas_call(
        _gla_proj_kernel,
        grid=(n // tm,),
        in_specs=[row, _resident(w_t.shape), _resident(w_a2.shape), _resident((1, GLA_DK))],
        out_specs=[half, half, row, row, half],
        out_shape=[jax.ShapeDtypeStruct((n, GLA_DK), F32), jax.ShapeDtypeStruct((n, GLA_DK), F32),
                   jax.ShapeDtypeStruct((n, GLA_DV), BF16), jax.ShapeDtypeStruct((n, GLA_DV), F32),
                   jax.ShapeDtypeStruct((n, GLA_DK), F32)],
        compiler_params=_params("parallel"),
        name="gla_proj",
    )(x, w_t, w_a2, b_a)


def _row_to_col(row):
    n = row.shape[1]
    eye = lax.broadcasted_iota(jnp.int32, (n, n), 0) == lax.broadcasted_iota(jnp.int32, (n, n), 1)
    return jnp.sum(jnp.where(eye, row, 0.0), axis=1, keepdims=True)


def _gla_chunk_kernel(q_ref, k_ref, v_ref, la_ref, tri_ref, o_ref, s_out_ref, s_ref, *, n_chunks):
    t = pl.program_id(2)

    @pl.when(t == 0)
    def _():
        s_ref[...] = jnp.zeros_like(s_ref)

    c = GLA_CHUNK
    ts = n_chunks * c
    chunk_rows = [slice(n * c, (n + 1) * c) for n in range(n_chunks)]
    bc = jnp.concatenate([_tri_dot(tri_ref[...], la_ref[r, :]) for r in chunk_rows], axis=0)
    b_last = [bc[r.stop - 1:r.stop, :] for r in chunk_rows]
    b_last_rows = jnp.concatenate([jnp.broadcast_to(b, (c, b.shape[1])) for b in b_last], axis=0)
    k = k_ref[...]
    v = v_ref[...]
    q_dec = (q_ref[...] * GLA_DK_H ** -0.5 * jnp.exp(bc)).astype(BF16)
    k_inv = (k * jnp.exp(-bc)).astype(BF16)
    k_end = (k * jnp.exp(b_last_rows - bc)).astype(BF16)
    row = lax.broadcasted_iota(jnp.int32, (ts, ts), 0)
    col = lax.broadcasted_iota(jnp.int32, (ts, ts), 1)
    same_chunk_causal = (col <= row) & (col >= (row & -c))
    attn = jnp.where(same_chunk_causal, _dot_nt(q_dec, k_inv), 0.0).astype(BF16)
    o_intra = _dot(attn, v)
    kv = [_dot_tn(k_end[r, :], v[r, :]) for r in chunk_rows]
    decay = [_row_to_col(jnp.exp(b)) for b in b_last]
    s = s_ref[...]
    s_before = []
    for n in range(n_chunks):
        s_before.append(s.astype(BF16))
        s = decay[n] * s + kv[n]
    s_ref[...] = s
    for n, r in enumerate(chunk_rows):
        o_ref[r, :] = o_intra[r, :] + _dot(q_dec[r, :], s_before[n])

    @pl.when(t == pl.num_programs(2) - 1)
    def _():
        s_out_ref[0, 0] = s


def _gla_chunked(q, k, v, la, batch, seq, ts):
    nt = seq // ts
    n_chunks = ts // GLA_CHUNK
    tri = jnp.asarray(np.tril(np.ones((GLA_CHUNK, GLA_CHUNK), np.float32)), BF16)
    key = pl.BlockSpec((ts, GLA_DK_H), lambda b, h, t: (b * nt + t, h))
    val = pl.BlockSpec((ts, GLA_DV_H), lambda b, h, t: (b * nt + t, h))
    return pl.pallas_call(
        functools.partial(_gla_chunk_kernel, n_chunks=n_chunks),
        grid=(batch, GLA_HEADS, nt),
        in_specs=[key, key, val, key, _resident((GLA_CHUNK, GLA_CHUNK))],
        out_specs=[val, pl.BlockSpec((1, 1, GLA_DK_H, GLA_DV_H), lambda b, h, t: (b, h, 0, 0))],
        out_shape=[jax.ShapeDtypeStruct((batch * seq, GLA_DV), F32),
                   jax.ShapeDtypeStruct((batch, GLA_HEADS, GLA_DK_H, GLA_DV_H), F32)],
        scratch_shapes=[pltpu.VMEM((GLA_DK_H, GLA_DV_H), F32)],
        compiler_params=_params("parallel", "parallel", "arbitrary"),
        name="gla_chunked",
    )(q, k, v, la, tri)


def _gla_step_kernel(q_ref, k_ref, v_ref, la_ref, s0_ref, o_ref, s_ref):
    for h in range(GLA_HEADS):
        la = la_ref[0, h]
        k = k_ref[0, h]
        v = v_ref[0, h]
        s0 = s0_ref[0, h]
        decay = jnp.exp(la)
        q_dec = (q_ref[0, h] * GLA_DK_H ** -0.5 * decay).astype(BF16)
        k_inv = (k * jnp.exp(-la)).astype(BF16)
        attn = jnp.sum(q_dec.astype(F32) * k_inv.astype(F32), axis=1, keepdims=True).astype(BF16).astype(F32)
        q_rows = jnp.broadcast_to(q_dec.astype(F32), (SUBLANES, GLA_DK_H)).astype(BF16)
        o_ref[0, h] = attn * v + _dot(q_rows, s0.astype(BF16))[0:1, :]
        s_ref[0, h] = _row_to_col(decay) * s0 + _row_to_col(k.astype(BF16).astype(F32)) * v


def _gla_step(q, k, v, la, s0):
    nb = q.shape[0]
    heads = lambda a, width: a.astype(F32).reshape(nb, GLA_HEADS, 1, width)
    key = pl.BlockSpec((1, GLA_HEADS, 1, GLA_DK_H), lambda b: (b, 0, 0, 0))
    val = pl.BlockSpec((1, GLA_HEADS, 1, GLA_DV_H), lambda b: (b, 0, 0, 0))
    state = pl.BlockSpec((1, GLA_HEADS, GLA_DK_H, GLA_DV_H), lambda b: (b, 0, 0, 0))
    o, s = pl.pallas_call(
        _gla_step_kernel,
        grid=(nb,),
        in_specs=[key, key, val, key, state],
        out_specs=[val, state],
        out_shape=[jax.ShapeDtypeStruct((nb, GLA_HEADS, 1, GLA_DV_H), F32),
                   jax.ShapeDtypeStruct((nb, GLA_HEADS, GLA_DK_H, GLA_DV_H), F32)],
        compiler_params=_params("parallel"),
        name="gla_step",
    )(heads(q, GLA_DK_H), heads(k, GLA_DK_H), heads(v, GLA_DV_H), heads(la, GLA_DK_H), s0)
    return o.reshape(nb, GLA_DV), s


PROMPT_TILE = 512
WIDE_TILE = 1024
ATTN_Q_TILE = 512
ATTN_K_TILE = 256
GLA_TILE = 512


def kernel(x_prompt, x_sample, cache_fox_k, cache_fox_v, cache_fox_logf, state_gla, page_table,
           ln_g, ln_b, ffn_w_in, ffn_w_out, fox_w_in, fox_b_f, fox_w_o,
           gla_w_in, gla_w_a2, gla_b_a, gla_norm_g, gla_w_o):
    batch, seq, _ = x_prompt.shape
    nb = x_sample.shape[0]
    xp = x_prompt.reshape(batch * seq, D_MODEL)
    xs = x_sample.reshape(nb, D_MODEL)
    tp = min(PROMPT_TILE, batch * seq)
    tw = min(WIDE_TILE, batch * seq)
    vec = lambda a: a.reshape(1, -1)
    w_in = ffn_w_in.astype(BF16)
    w_out = ffn_w_out.astype(BF16)

    def ffn(i, half, ln):
        return ((w_in, (i, half)), (w_out, (i, half)), vec(ln_g[i, ln]), vec(ln_b[i, ln]))

    xp, xs = _ffn_half(xp, xs, *ffn(0, 0, 0), tw)
    w_fox_t = fox_w_in[0].T.astype(BF16)
    kt, vt, vtb, lft, qa, ka = _fox_proj_t(xp, w_fox_t, vec(fox_b_f[0]), batch, seq, min(PROMPT_TILE, seq))
    qs, ks, vs, lfs = _fox_proj(xs, w_fox_t, vec(fox_b_f[0]), nb)
    mp, ms = _fox_mix(qa, ka, vtb, batch, seq, min(ATTN_Q_TILE, seq), min(ATTN_K_TILE, seq),
                      qs, ks, vs, lfs, cache_fox_k[0], cache_fox_v[0], cache_fox_logf[0], page_table)
    mix = (fox_w_o[0].astype(BF16), vec(ln_g[0, 1]), vec(ln_b[0, 1])) + ffn(0, 1, 2)
    xp, xs = _mix_ffn(xp, mp, xs, ms, *mix, tp)

    xp, xs = _ffn_half(xp, xs, *ffn(1, 0, 0), tw)
    proj = (gla_w_in[0].T.astype(BF16), gla_w_a2[0].astype(BF16), vec(gla_b_a[0]))
    gqp, gkp, gvp, grp, glap = _gla_proj(xp, *proj, tw)
    gqs, gks, gvs, grs, glas = _gla_proj(xs, *proj, nb)
    op, state_p = _gla_chunked(gqp, gkp, gvp, glap, batch, seq, min(GLA_TILE, seq))
    os_, state_s = _gla_step(gqs, gks, gvs, glas, state_gla[0])
    mix = (vec(gla_norm_g[0]), gla_w_o[0].astype(BF16), vec(ln_g[1, 1]), vec(ln_b[1, 1])) + ffn(1, 1, 2)
    xp, xs = _gla_mix_ffn(xp, op, grp, xs, os_, grs, *mix, tp)

    heads = (FOX_HEADS, FOX_HEAD_DIM)
    token_major = lambda a: jnp.transpose(a.reshape(batch, *heads, seq), (0, 3, 1, 2))[None]
    return (xp.reshape(batch, seq, D_MODEL), xs.reshape(nb, 1, D_MODEL),
            token_major(kt), token_major(vt), jnp.transpose(lft, (0, 2, 1))[None],
            ks.reshape(1, nb, 1, *heads), vs.reshape(1, nb, 1, *heads), lfs.reshape(1, nb, 1, FOX_HEADS),
            state_p[None], state_s[None])
```
